```python
import jax, jax.numpy as jnp
from jax import lax
import numpy as np

D_MODEL = 1024
BATCH = 16
SEQ = 2048
DEPTH = 4

N_MIXERS = 2
N_ML_LAYERS = (DEPTH + 1) // 2
N_NSA_LAYERS = DEPTH // 2
RMS_EPS = 1e-6
LN_EPS = 1e-6
NEG_INF = -1e30

ML_INNER = 2 * D_MODEL
ML_HEADS = 4
ML_DH = ML_INNER // ML_HEADS
ML_CONV = 4
ML_QKV_BLK = 4
ML_NBLK = ML_INNER // ML_QKV_BLK
ML_CHUNK = 64

NSA_HEADS = 16
NSA_GROUPS = 4
NSA_HPG = NSA_HEADS // NSA_GROUPS
NSA_DK = 64
NSA_DV = 64
NSA_INNER = NSA_HEADS * NSA_DV
NSA_PROJ = NSA_HEADS * NSA_DK + 6 * NSA_GROUPS * NSA_DK + NSA_INNER + 3 * NSA_HEADS
ROPE_DIM = NSA_DK // 4
ROPE_THETA = 500000.0
CMP_BLOCK = 32
CMP_STRIDE = 16
CMP_HIDDEN = 2 * NSA_DK
SEL_BLOCK = 64
SEL_TOPK = 16
SEL_LOCAL = 2
SEL_FORCE = 1e9
SEL_Q_BLOCK = 16
WINDOW = 512
WIN_Q_BLOCK = 128

kernel_name = 'mlstm_nsa_interleaved_hybrid'


def rmsnorm(x, g):
    xf = x.astype(jnp.float32)
    y = xf * lax.rsqrt(jnp.mean(xf * xf, axis=-1, keepdims=True) + RMS_EPS)
    return y.astype(x.dtype) * g


def rope_partial(t, cos, sin):
    half = ROPE_DIM // 2
    shape = (t.shape[1],) + (1,) * (t.ndim - 3) + (half,)
    c = cos.reshape(shape).astype(t.dtype)
    s = sin.reshape(shape).astype(t.dtype)
    t1, t2, rest = t[..., :half], t[..., half:ROPE_DIM], t[..., ROPE_DIM:]
    return jnp.concatenate([t1 * c - t2 * s, t2 * c + t1 * s, rest], axis=-1)


def masked_softmax(s, mask):
    s = jnp.where(mask, s.astype(jnp.float32), NEG_INF)
    return jax.nn.softmax(s, axis=-1)


def mlstm_chunkwise(q, k, v, i_pre, log_f):
    B, NH, S, DH = q.shape
    nc = S // ML_CHUNK

    def to_chunks(t):
        return jnp.moveaxis(t.reshape(t.shape[:2] + (nc, ML_CHUNK) + t.shape[3:]), 2, 0)

    causal = jnp.tril(jnp.ones((ML_CHUNK, ML_CHUNK), dtype=bool))

    def step(carry, xs):
        C, n, m = carry
        qc, kc, vc, ic, fc = xs
        b = jnp.cumsum(fc, axis=-1)
        dmat = jnp.where(causal, b[..., :, None] - b[..., None, :] + ic[..., None, :], -jnp.inf)
        inter = b + m[..., None]
        m_row = jnp.maximum(inter, dmat.max(-1))
        w_intra = jnp.exp(dmat - m_row[..., None])
        w_inter = jnp.exp(inter - m_row)
        s = jnp.einsum('bhid,bhjd->bhij', qc, kc) * w_intra
        num = jnp.einsum('bhij,bhjd->bhid', s, vc) + w_inter[..., None] * jnp.einsum('bhvk,bhik->bhiv', C, qc)
        den = s.sum(-1) + w_inter * jnp.einsum('bhk,bhik->bhi', n, qc)
        h = num / jnp.maximum(jnp.abs(den), jnp.exp(-m_row))[..., None]
        b_last = b[..., -1]
        a = b_last[..., None] - b + ic
        m_new = jnp.maximum(b_last + m, a.max(-1))
        w_state = jnp.exp(a - m_new[..., None])
        decay = jnp.exp(b_last + m - m_new)
        C = decay[..., None, None] * C + jnp.einsum('bhjv,bhjk->bhvk', vc * w_state[..., None], kc)
        n = decay[..., None] * n + jnp.einsum('bhj,bhjk->bhk', w_state, kc)
        return (C, n, m_new), h

    init = (jnp.zeros((B, NH, DH, DH), jnp.float32),
            jnp.zeros((B, NH, DH), jnp.float32),
            jnp.zeros((B, NH), jnp.float32))
    _, h = lax.scan(step, init, tuple(to_chunks(t) for t in (q, k, v, i_pre, log_f)))
    return jnp.moveaxis(h, 0, 2).reshape(B, NH, S, DH)


def mlstm_mixer(h, w_in, conv_w, conv_b, w_q, w_k, w_v, w_if, b_if, ln_w, skip, w_out):
    B, S, _ = h.shape
    x_in, z = jnp.split(h @ w_in, 2, axis=-1)
    x_pad = jnp.pad(x_in, ((0, 0), (ML_CONV - 1, 0), (0, 0)))
    conv = sum(x_pad[:, tap:tap + S] * conv_w[tap] for tap in range(ML_CONV))
    x_conv = jax.nn.silu(conv + conv_b)

    def headwise(t, w):
        return jnp.einsum('bsnj,nij->bsni', t.reshape(B, S, ML_NBLK, ML_QKV_BLK), w).reshape(B, S, ML_INNER)

    q = headwise(x_conv, w_q)
    k = headwise(x_conv, w_k)
    v = headwise(x_in, w_v)
    gates = (jnp.concatenate([q, k, v], axis=-1) @ w_if + b_if).astype(jnp.float32)
    i_pre, f_pre = jnp.split(gates, 2, axis=-1)

    def heads(t):
        return t.reshape(B, S, ML_HEADS, ML_DH).transpose(0, 2, 1, 3).astype(jnp.float32)

    hh = mlstm_chunkwise(heads(q) * ML_DH ** -0.5, heads(k), heads(v),
                         i_pre.transpose(0, 2, 1), jax.nn.log_sigmoid(f_pre).transpose(0, 2, 1))
    mu = hh.mean(-1, keepdims=True)
    var = jnp.mean(jnp.square(hh - mu), axis=-1, keepdims=True)
    hn = ((hh - mu) * lax.rsqrt(var + LN_EPS)).transpose(0, 2, 1, 3).reshape(B, S, ML_INNER)
    hn = hn.astype(h.dtype) * ln_w
    return ((hn + skip * x_conv) * jax.nn.silu(z)) @ w_out


def nsa_mixer(h, w_in, b_gate, cmp_pe, cmp_w1, cmp_w2, w_out, cos, sin):
    B, S, _ = h.shape
    G, HPG, DK, DV = NSA_GROUPS, NSA_HPG, NSA_DK, NSA_DV
    sizes = [NSA_HEADS * DK] + [G * DK] * 6 + [NSA_INNER]
    q, k_c, v_c, k_s, v_s, k_w, v_w, z, g_logit = jnp.split(h @ w_in, np.cumsum(sizes).tolist(), axis=-1)
    q = q.reshape(B, S, G, HPG, DK)
    k_c, v_c, k_s, v_s, k_w, v_w = (t.reshape(B, S, G, DK) for t in (k_c, v_c, k_s, v_s, k_w, v_w))
    scale = DK ** -0.5
    pos = jnp.arange(S)

    n_cmp = (S - CMP_BLOCK) // CMP_STRIDE + 1
    cmp_start = jnp.arange(n_cmp) * CMP_STRIDE
    tok_idx = cmp_start[:, None] + jnp.arange(CMP_BLOCK)[None, :]

    def compress(t, pe, w1, w2):
        blocks = t[:, tok_idx] + pe[:, None, :]
        flat = blocks.transpose(0, 1, 3, 2, 4).reshape(B, n_cmp, G, CMP_BLOCK * DK)
        return jax.nn.silu(flat @ w1) @ w2

    k_cmp = compress(k_c, cmp_pe[0], cmp_w1[0], cmp_w2[0])
    v_cmp = compress(v_c, cmp_pe[1], cmp_w1[1], cmp_w2[1])
    cmp_mask = (cmp_start + CMP_BLOCK - 1)[None, :] <= pos[:, None]
    p_cmp = masked_softmax(jnp.einsum('bsghd,bcgd->bghsc', q, k_cmp) * scale, cmp_mask)
    p_cmp = p_cmp * cmp_mask.any(-1)[:, None].astype(jnp.float32)
    o_cmp = jnp.einsum('bghsc,bcgd->bsghd', p_cmp.astype(v_cmp.dtype), v_cmp)

    q_r = rope_partial(q, cos, sin)
    k_s = rope_partial(k_s, cos, sin)
    k_w = rope_partial(k_w, cos, sin)

    n_sel = S // SEL_BLOCK
    sel_start = jnp.arange(n_sel) * SEL_BLOCK
    overlap = jnp.clip(jnp.minimum(cmp_start[:, None] + CMP_BLOCK, sel_start[None, :] + SEL_BLOCK)
                       - jnp.maximum(cmp_start[:, None], sel_start[None, :]), 0, None)
    overlap = overlap.astype(jnp.float32) / CMP_STRIDE
    importance = jnp.einsum('bghsc,cn->bgsn', p_cmp, overlap)
    blk = jnp.arange(n_sel)
    dist = (pos // SEL_BLOCK)[:, None] - blk[None, :]
    forced = (blk[None, :] == 0) | ((dist >= 0) & (dist < SEL_LOCAL))
    importance = jnp.where(forced, SEL_FORCE, jnp.where(dist >= 0, importance, -1.0))
    top_k = min(SEL_TOPK, n_sel)
    top_val, top_idx = lax.top_k(importance, top_k)
    ks_blk = k_s.transpose(0, 2, 1, 3).reshape(B, G, n_sel, SEL_BLOCK, DK)
    vs_blk = v_s.transpose(0, 2, 1, 3).reshape(B, G, n_sel, SEL_BLOCK, DV)
    QB = SEL_Q_BLOCK
    n_qb = S // QB
    bi = jnp.arange(B)[:, None, None]
    gi = jnp.arange(G)[None, :, None]

    def sel_step(args):
        qb, idx, ok, qpos = args
        flat = idx.reshape(B, G, QB * top_k)
        kg = ks_blk[bi, gi, flat].reshape(B, G, QB, top_k * SEL_BLOCK, DK)
        vg = vs_blk[bi, gi, flat].reshape(B, G, QB, top_k * SEL_BLOCK, DV)
        kpos = idx[..., None] * SEL_BLOCK + jnp.arange(SEL_BLOCK)
        mask = (ok[..., None] & (kpos <= qpos[:, None, None])).reshape(B, G, 1, QB, top_k * SEL_BLOCK)
        p = masked_softmax(jnp.einsum('bqghd,bgqkd->bghqk', qb, kg) * scale, mask)
        return jnp.einsum('bghqk,bgqkd->bqghd', p.astype(vg.dtype), vg)

    xs = (jnp.moveaxis(q_r.reshape(B, n_qb, QB, G, HPG, DK), 1, 0),
          jnp.moveaxis(top_idx.reshape(B, G, n_qb, QB, top_k), 2, 0),
          jnp.moveaxis((top_val >= 0).reshape(B, G, n_qb, QB, top_k), 2, 0),
          pos.reshape(n_qb, QB))
    o_sel = jnp.moveaxis(lax.map(sel_step, xs), 0, 1).reshape(B, S, G, HPG, DV)

    k_wp = jnp.pad(k_w, ((0, 0), (WINDOW, 0), (0, 0), (0, 0)))
    v_wp = jnp.pad(v_w, ((0, 0), (WINDOW, 0), (0, 0), (0, 0)))
    span = WIN_Q_BLOCK + WINDOW

    def win_step(start):
        qb = lax.dynamic_slice_in_dim(q_r, start, WIN_Q_BLOCK, axis=1)
        kb = lax.dynamic_slice_in_dim(k_wp, start, span, axis=1)
        vb = lax.dynamic_slice_in_dim(v_wp, start, span, axis=1)
        kpos = start - WINDOW + jnp.arange(span)
        diff = (start + jnp.arange(WIN_Q_BLOCK))[:, None] - kpos[None, :]
        mask = (diff >= 0) & (diff < WINDOW) & (kpos >= 0)[None, :]
        p = masked_softmax(jnp.einsum('bqghd,bkgd->bghqk', qb, kb) * scale, mask)
        return jnp.einsum('bghqk,bkgd->bqghd', p.astype(vb.dtype), vb)

    o_win = lax.map(win_step, jnp.arange(S // WIN_Q_BLOCK) * WIN_Q_BLOCK)
    o_win = jnp.moveaxis(o_win, 0, 1).reshape(B, S, G, HPG, DV)

    gate = jax.nn.sigmoid((g_logit + b_gate).astype(jnp.float32)).reshape(B, S, NSA_HEADS, 3).astype(h.dtype)

    def heads(o):
        return o.reshape(B, S, NSA_HEADS, DV)

    o = gate[..., 0:1] * heads(o_cmp) + gate[..., 1:2] * heads(o_sel) + gate[..., 2:3] * heads(o_win)
    return (o.reshape(B, S, NSA_INNER) * jax.nn.silu(z)) @ w_out


def setup_inputs(seed: int = 0) -> dict:
    key = jax.random.key(seed)
    ks = jax.random.split(key, 24)
    f32 = jnp.float32

    def nrm(k, shape, fan):
        return jax.random.normal(k, shape, f32) * fan ** -0.5

    def gain(k, shape):
        return 1.0 + 0.02 * jax.random.normal(k, shape, f32)

    def small(k, shape, s=0.02):
        return s * jax.random.normal(k, shape, f32)

    LM, LN = N_ML_LAYERS, N_NSA_LAYERS
    b_if = jnp.concatenate([small(ks[10], (LM, ML_HEADS), 0.1),
                            jnp.linspace(3.0, 6.0, ML_HEADS, dtype=f32)[None, :]
                            + small(ks[11], (LM, ML_HEADS), 0.01)], axis=-1)
    return {
        'x': jax.random.normal(ks[0], (BATCH, SEQ, D_MODEL), f32),
        'ml_norm': gain(ks[1], (LM, D_MODEL)),
        'ml_w_in': nrm(ks[2], (LM, D_MODEL, 2 * ML_INNER), D_MODEL),
        'ml_conv_w': nrm(ks[3], (LM, ML_CONV, ML_INNER), ML_CONV),
        'ml_conv_b': small(ks[4], (LM, ML_INNER)),
        'ml_w_q': nrm(ks[5], (LM, ML_NBLK, ML_QKV_BLK, ML_QKV_BLK), ML_QKV_BLK),
        'ml_w_k': nrm(ks[6], (LM, ML_NBLK, ML_QKV_BLK, ML_QKV_BLK), ML_QKV_BLK),
        'ml_w_v': nrm(ks[7], (LM, ML_NBLK, ML_QKV_BLK, ML_QKV_BLK), ML_QKV_BLK),
        'ml_w_if': nrm(ks[8], (LM, 3 * ML_INNER, 2 * ML_HEADS), 3 * ML_INNER),
        'ml_b_if': b_if,
        'ml_ln_w': gain(ks[12], (LM, ML_INNER)),
        'ml_skip': gain(ks[13], (LM, ML_INNER)),
        'ml_w_out': nrm(ks[14], (LM, ML_INNER, D_MODEL), ML_INNER),
        'nsa_norm': gain(ks[15], (LN, D_MODEL)),
        'nsa_w_in': nrm(ks[16], (LN, D_MODEL, NSA_PROJ), D_MODEL),
        'nsa_b_gate': small(ks[17], (LN, 3 * NSA_HEADS)),
        'nsa_cmp_pe': small(ks[18], (LN, 2, CMP_BLOCK, NSA_DK), 0.1),
        'nsa_cmp_w1': nrm(ks[19], (LN, 2, CMP_BLOCK * NSA_DK, CMP_HIDDEN), CMP_BLOCK * NSA_DK),
        'nsa_cmp_w2': nrm(ks[20], (LN, 2, CMP_HIDDEN, NSA_DK), CMP_HIDDEN),
        'nsa_w_out': nrm(ks[21], (LN, NSA_INNER, D_MODEL), NSA_INNER),
        'final_norm': gain(ks[22], (D_MODEL,)),
    }


def reference(x, ml_norm, ml_w_in, ml_conv_w, ml_conv_b, ml_w_q, ml_w_k, ml_w_v, ml_w_if, ml_b_if,
              ml_ln_w, ml_skip, ml_w_out, nsa_norm, nsa_w_in, nsa_b_gate, nsa_cmp_pe, nsa_cmp_w1,
              nsa_cmp_w2, nsa_w_out, final_norm):
    S = x.shape[1]
    pos_f = jnp.arange(S, dtype=jnp.float32)
    inv_freq = ROPE_THETA ** (-jnp.arange(0, ROPE_DIM, 2, dtype=jnp.float32) / ROPE_DIM)
    ang = pos_f[:, None] * inv_freq[None, :]
    cos, sin = jnp.cos(ang), jnp.sin(ang)
    for i in range(DEPTH):
        j = i // N_MIXERS
        if i % N_MIXERS == 0:
            x = x + mlstm_mixer(rmsnorm(x, ml_norm[j]), ml_w_in[j], ml_conv_w[j], ml_conv_b[j],
                                ml_w_q[j], ml_w_k[j], ml_w_v[j], ml_w_if[j], ml_b_if[j],
                                ml_ln_w[j], ml_skip[j], ml_w_out[j])
        else:
            x = x + nsa_mixer(rmsnorm(x, nsa_norm[j]), nsa_w_in[j], nsa_b_gate[j], nsa_cmp_pe[j],
                              nsa_cmp_w1[j], nsa_cmp_w2[j], nsa_w_out[j], cos, sin)
    return rmsnorm(x, final_norm)
```

```python
import functools

import jax
import jax.numpy as jnp
from jax import lax
from jax.experimental import pallas as pl
from jax.experimental.pallas import tpu as pltpu

F32 = jnp.float32
BF16 = jnp.bfloat16

RMS_EPS = 1e-6
LN_EPS = 1e-6
NEG = -1e30

ML_HEADS = 4
ML_CONV = 4
ML_QKV_BLK = 4
ML_CHUNK = 256
ML_ROWS = 256

NSA_HEADS = 16
NSA_GROUPS = 4
NSA_HPG = NSA_HEADS // NSA_GROUPS
NSA_DK = 64
ROPE_DIM = NSA_DK // 4
ROPE_THETA = 500000.0
CMP_BLOCK = 32
CMP_STRIDE = 16
SEL_BLOCK = 64
SEL_TOPK = 16
SEL_LOCAL = 2
SEL_FORCE = 1e9
WINDOW = 512
NSA_ROWS = 256
ATT_TILE = 128
GATE_PAD = 16

LANES = 128
VMEM_LIMIT = 56 * 1024 * 1024


def _dot(a, b):
    return jnp.dot(a, b, preferred_element_type=F32)


def _sigmoid(v):
    return 1.0 / (1.0 + jnp.exp(-v))


def _const_spec(shape):
    zeros = (0,) * len(shape)
    return pl.BlockSpec(shape, lambda *_: zeros)


def _params(sem):
    return pltpu.CompilerParams(dimension_semantics=sem, vmem_limit_bytes=VMEM_LIMIT)


def _rmsnorm(x, g):
    ms = jnp.mean(x * x, axis=-1, keepdims=True)
    return (x * lax.rsqrt(ms + RMS_EPS)) * g


def _ml_front_kernel(x_ref, g_ref, win_ref, convw_ref, convb_ref, wq_ref, wk_ref, wv_ref, wif_ref, bif_ref,
                     q_ref, k_ref, kT_ref, v_ref, xc_ref, z_ref, gcol_ref, grow_ref, xbuf,
                     *, tm, tiles_per_seq, inner, q_scale):
    t = pl.program_id(0)
    h = _rmsnorm(x_ref[...], g_ref[...])
    xz = _dot(h.astype(BF16), win_ref[...])
    x_in = xz[:, :inner]
    z_ref[...] = xz[:, inner:].astype(BF16)

    @pl.when(t % tiles_per_seq == 0)
    def _():
        xbuf[0:8, :] = jnp.zeros((8, inner), F32)

    xbuf[8:8 + tm, :] = x_in
    conv = jnp.broadcast_to(convb_ref[...], (tm, inner))
    for tap in range(ML_CONV):
        lo = 8 - (ML_CONV - 1) + tap
        conv = conv + xbuf[lo:lo + tm, :] * convw_ref[tap:tap + 1, :]
    xbuf[0:8, :] = xbuf[tm:tm + 8, :]
    xc = conv * _sigmoid(conv)
    xc_b = xc.astype(BF16)
    xc_ref[...] = xc_b
    xin_b = x_in.astype(BF16)

    gates = jnp.broadcast_to(bif_ref[...], (tm, LANES))
    for c in range(inner // LANES):
        sl = slice(c * LANES, (c + 1) * LANES)
        qc = _dot(xc_b[:, sl], wq_ref[c])
        kc = _dot(xc_b[:, sl], wk_ref[c])
        vc = _dot(xin_b[:, sl], wv_ref[c])
        kc_b = kc.astype(BF16)
        vc_b = vc.astype(BF16)
        q_ref[:, sl] = (qc * q_scale).astype(BF16)
        k_ref[:, sl] = kc_b
        kT_ref[sl, :] = kc.T.astype(BF16)
        v_ref[:, sl] = vc_b
        gates = gates + _dot(qc.astype(BF16), wif_ref[sl, :])
        gates = gates + _dot(kc_b, wif_ref[inner + c * LANES:inner + (c + 1) * LANES, :])
        gates = gates + _dot(vc_b, wif_ref[2 * inner + c * LANES:2 * inner + (c + 1) * LANES, :])

    logsig = -(jnp.maximum(-gates, 0.0) + jnp.log1p(jnp.exp(-jnp.abs(gates))))
    lane = lax.broadcasted_iota(jnp.int32, (tm, LANES), 1)
    gsel = jnp.where(lane < ML_HEADS, gates, logsig)
    gcol_ref[...] = gsel[:, 0:2 * ML_HEADS]
    grow_ref[...] = gsel.T[0:2 * ML_HEADS, :]


def _ml_front(x2, g, win, convw, convb, wq, wk, wv, wif, bif, *, seq):
    T, D = x2.shape
    inner = win.shape[1] // 2
    tm = ML_ROWS
    nq = inner // LANES
    kern = functools.partial(_ml_front_kernel, tm=tm, tiles_per_seq=seq // tm, inner=inner,
                             q_scale=float((inner // ML_HEADS) ** -0.5))
    row = lambda w: pl.BlockSpec((tm, w), lambda t: (t, 0))
    return pl.pallas_call(
        kern,
        grid=(T // tm,),
        in_specs=[row(D), _const_spec((1, D)), _const_spec((D, 2 * inner)),
                  _const_spec((ML_CONV, inner)), _const_spec((1, inner)),
                  _const_spec((nq, LANES, LANES)), _const_spec((nq, LANES, LANES)), _const_spec((nq, LANES, LANES)),
                  _const_spec((3 * inner, LANES)), _const_spec((1, LANES))],
        out_specs=[row(inner), row(inner), pl.BlockSpec((inner, tm), lambda t: (0, t)), row(inner),
                   row(inner), row(inner), row(2 * ML_HEADS),
                   pl.BlockSpec((2 * ML_HEADS, tm), lambda t: (0, t))],
        out_shape=[jax.ShapeDtypeStruct((T, inner), BF16), jax.ShapeDtypeStruct((T, inner), BF16),
                   jax.ShapeDtypeStruct((inner, T), BF16), jax.ShapeDtypeStruct((T, inner), BF16),
                   jax.ShapeDtypeStruct((T, inner), BF16), jax.ShapeDtypeStruct((T, inner), BF16),
                   jax.ShapeDtypeStruct((T, 2 * ML_HEADS), F32), jax.ShapeDtypeStruct((2 * ML_HEADS, T), F32)],
        scratch_shapes=[pltpu.VMEM((tm + 8, inner), F32)],
        compiler_params=_params(("arbitrary",)),
        name="ml_front",
    )(x2, g, win, convw, convb, wq, wk, wv, wif, bif)


def _ml_core_kernel(q_ref, k_ref, kT_ref, v_ref, gcol_ref, grow_ref, lnw_ref, o_ref, s_sc, n_sc, m_sc,
                    *, L, dh):
    c = pl.program_id(1)

    @pl.when(c == 0)
    def _():
        s_sc[...] = jnp.zeros_like(s_sc)
        n_sc[...] = jnp.zeros_like(n_sc)
        m_sc[...] = jnp.zeros_like(m_sc)

    gcol = gcol_ref[...]
    grow = grow_ref[...]
    row = lax.broadcasted_iota(jnp.int32, (L, L), 0)
    col = lax.broadcasted_iota(jnp.int32, (L, L), 1)
    causal = col <= row
    for h in range(ML_HEADS):
        sl = slice(h * dh, (h + 1) * dh)
        q = q_ref[:, sl]
        k = k_ref[:, sl]
        v = v_ref[:, sl]
        kT = kT_ref[sl, :]
        ic = gcol[:, h:h + 1]
        fc = gcol[:, ML_HEADS + h:ML_HEADS + h + 1]
        ir = grow[h:h + 1, :]
        fr = grow[ML_HEADS + h:ML_HEADS + h + 1, :]
        m_prev = m_sc[h][:, 0:1]
        b_col = jnp.sum(jnp.where(causal, fr, 0.0), axis=1, keepdims=True)
        b_row = jnp.sum(jnp.where(row <= col, fc, 0.0), axis=0, keepdims=True)
        dmat = jnp.where(causal, b_col - b_row + ir, -jnp.inf)
        inter = b_col + m_prev
        m_row = jnp.maximum(inter, jnp.max(dmat, axis=1, keepdims=True))
        w_intra = jnp.exp(dmat - m_row)
        w_inter = jnp.exp(inter - m_row)
        s = _dot(q, kT) * w_intra
        num = _dot(s.astype(BF16), v) + w_inter * _dot(q, s_sc[h].astype(BF16))
        den = jnp.sum(s, axis=1, keepdims=True) + w_inter * jnp.sum(q.astype(F32) * n_sc[h], axis=1, keepdims=True)
        hh = num * (1.0 / jnp.maximum(jnp.abs(den), jnp.exp(-m_row)))
        mu = jnp.mean(hh, axis=1, keepdims=True)
        cen = hh - mu
        var = jnp.mean(cen * cen, axis=1, keepdims=True)
        o_ref[:, sl] = ((cen * lax.rsqrt(var + LN_EPS)) * lnw_ref[:, sl]).astype(BF16)

        b_last = b_col[L - 1:L, :]
        a_row = b_last - b_row + ir
        a_col = b_last - b_col + ic
        m_new = jnp.maximum(b_last + m_prev, jnp.max(a_row, axis=1, keepdims=True))
        w_row = jnp.exp(a_row - m_new)
        w_col = jnp.exp(a_col - m_new)
        decay = jnp.exp(b_last + m_prev - m_new)
        s_sc[h] = decay * s_sc[h] + _dot((kT.astype(F32) * w_row).astype(BF16), v)
        n_sc[h] = decay * n_sc[h] + jnp.sum(k.astype(F32) * w_col, axis=0, keepdims=True)
        m_sc[h] = jnp.broadcast_to(m_new, (1, LANES))


def _ml_core(q, k, kT, v, gcol, grow, lnw, *, batch, seq):
    T, inner = q.shape
    L = ML_CHUNK
    nc = seq // L
    dh = inner // ML_HEADS
    kern = functools.partial(_ml_core_kernel, L=L, dh=dh)
    row = lambda w: pl.BlockSpec((L, w), lambda b, c: (b * nc + c, 0))
    colb = lambda r: pl.BlockSpec((r, L), lambda b, c: (0, b * nc + c))
    return pl.pallas_call(
        kern,
        grid=(batch, nc),
        in_specs=[row(inner), row(inner), colb(inner), row(inner), row(2 * ML_HEADS), colb(2 * ML_HEADS),
                  _const_spec((1, inner))],
        out_specs=row(inner),
        out_shape=jax.ShapeDtypeStruct((T, inner), BF16),
        scratch_shapes=[pltpu.VMEM((ML_HEADS, dh, dh), F32), pltpu.VMEM((ML_HEADS, 1, dh), F32),
                        pltpu.VMEM((ML_HEADS, 1, LANES), F32)],
        compiler_params=_params(("arbitrary", "arbitrary")),
        name="ml_core",
    )(q, k, kT, v, gcol, grow, lnw)


def _ml_out_kernel(hn_ref, xc_ref, z_ref, skip_ref, wout_ref, x_ref, o_ref):
    z = z_ref[...].astype(F32)
    u = (hn_ref[...].astype(F32) + skip_ref[...] * xc_ref[...].astype(F32)) * (z * _sigmoid(z))
    o_ref[...] = x_ref[...] + _dot(u.astype(BF16), wout_ref[...])


def _ml_out(hn, xc, z, skip, wout, x2):
    T, inner = hn.shape
    D = x2.shape[1]
    tm = ML_ROWS
    row = lambda w: pl.BlockSpec((tm, w), lambda t: (t, 0))
    return pl.pallas_call(
        _ml_out_kernel,
        grid=(T // tm,),
        in_specs=[row(inner), row(inner), row(inner), _const_spec((1, inner)), _const_spec((inner, D)), row(D)],
        out_specs=row(D),
        out_shape=jax.ShapeDtypeStruct((T, D), F32),
        input_output_aliases={5: 0},
        compiler_params=_params(("arbitrary",)),
        name="ml_out",
    )(hn, xc, z, skip, wout, x2)


def _rope(a, cos, s1, s2):
    return a * cos + pltpu.roll(a, ROPE_DIM // 2, 1) * s1 + pltpu.roll(a, LANES - ROPE_DIM // 2, 1) * s2


def _nsa_front_kernel(x_ref, g_ref, w_ref, bg_ref, cos_ref, s1_ref, s2_ref,
                      qc_ref, qr_ref, kc_ref, vc_ref, ks_ref, vsT_ref, kw_ref, vwT_ref, z_ref, gate_ref,
                      *, tm, hd, gd):
    h = _rmsnorm(x_ref[...], g_ref[...])
    xz = _dot(h.astype(BF16), w_ref[...])
    cos, s1, s2 = cos_ref[...], s1_ref[...], s2_ref[...]
    scale = NSA_DK ** -0.5
    G = NSA_GROUPS
    for c in range(hd // LANES):
        sl = slice(c * LANES, (c + 1) * LANES)
        a = xz[:, sl]
        qc_ref[:, sl] = (a * scale).astype(BF16)
        qr_ref[:, sl] = (_rope(a, cos, s1, s2) * scale).astype(BF16)

    def group_slabs(off, rope):
        slabs = [xz[:, off + c * LANES:off + (c + 1) * LANES] for c in range(gd // LANES)]
        if rope:
            slabs = [_rope(a, cos, s1, s2) for a in slabs]
        return slabs

    def store_rows(ref, slabs):
        for g in range(G):
            a = slabs[(g * NSA_DK) // LANES]
            lo = (g * NSA_DK) % LANES
            ref[0, g] = a[:, lo:lo + NSA_DK].astype(BF16)

    def store_cols(ref, slabs):
        for g in range(G):
            aT = slabs[(g * NSA_DK) // LANES].T
            lo = (g * NSA_DK) % LANES
            for j in range(tm // ATT_TILE):
                ref[0, g, j] = aT[lo:lo + NSA_DK, j * ATT_TILE:(j + 1) * ATT_TILE].astype(BF16)

    store_rows(kc_ref, group_slabs(hd, False))
    store_rows(vc_ref, group_slabs(hd + gd, False))
    store_rows(ks_ref, group_slabs(hd + 2 * gd, True))
    store_cols(vsT_ref, group_slabs(hd + 3 * gd, False))
    store_rows(kw_ref, group_slabs(hd + 4 * gd, True))
    store_cols(vwT_ref, group_slabs(hd + 5 * gd, False))
    zoff = hd + 6 * gd
    z_ref[...] = xz[:, zoff:zoff + hd].astype(BF16)
    gl = xz[:, zoff + hd:zoff + hd + LANES] + bg_ref[...]
    gT = _sigmoid(gl).T
    for g in range(G):
        gate_ref[g] = gT[g * GATE_PAD:(g + 1) * GATE_PAD, :]


def _nsa_front(x2, g, w, bg, cos, s1, s2, *, batch, seq):
    T, D = x2.shape
    tm = NSA_ROWS
    tps = seq // tm
    G = NSA_GROUPS
    hd = NSA_HEADS * NSA_DK
    gd = G * NSA_DK
    ntk = seq // ATT_TILE
    kern = functools.partial(_nsa_front_kernel, tm=tm, hd=hd, gd=gd)
    row = lambda w_: pl.BlockSpec((tm, w_), lambda t: (t, 0))
    tab = pl.BlockSpec((tm, LANES), lambda t: (t % tps, 0))
    rows4 = pl.BlockSpec((1, G, tm, NSA_DK), lambda t: (t // tps, 0, t % tps, 0))
    cols5 = pl.BlockSpec((1, G, tm // ATT_TILE, NSA_DK, ATT_TILE), lambda t: (t // tps, 0, t % tps, 0, 0))
    rows_shape = jax.ShapeDtypeStruct((batch, G, seq, NSA_DK), BF16)
    cols_shape = jax.ShapeDtypeStruct((batch, G, ntk, NSA_DK, ATT_TILE), BF16)
    return pl.pallas_call(
        kern,
        grid=(T // tm,),
        in_specs=[row(D), _const_spec((1, D)), _const_spec(w.shape), _const_spec((1, LANES)), tab, tab, tab],
        out_specs=[row(hd), row(hd), rows4, rows4, rows4, cols5, rows4, cols5, row(hd),
                   pl.BlockSpec((G, GATE_PAD, tm), lambda t: (0, 0, t))],
        out_shape=[jax.ShapeDtypeStruct((T, hd), BF16), jax.ShapeDtypeStruct((T, hd), BF16),
                   rows_shape, rows_shape, rows_shape, cols_shape, rows_shape, cols_shape,
                   jax.ShapeDtypeStruct((T, hd), BF16), jax.ShapeDtypeStruct((G, GATE_PAD, T), F32)],
        compiler_params=_params(("arbitrary",)),
        name="nsa_front",
    )(x2, g, w, bg, cos, s1, s2)


def _nsa_cmp_kernel(kc_ref, vc_ref, w1_ref, pe_ref, w2_ref, kcmp_ref, vcmpT_ref, *, hid):
    def mlp(x, i):
        ab = _dot(x, w1_ref[i])
        pb = _dot(pe_ref[i], w1_ref[i])
        bias = pb[0:1, 0:hid] + pb[8:9, hid:2 * hid]
        h1 = ab[:, 0:hid] + pltpu.roll(ab[:, hid:2 * hid], ab.shape[0] - 1, 0) + bias
        return _dot((h1 * _sigmoid(h1)).astype(BF16), w2_ref[i])

    kcmp_ref[0, 0] = mlp(kc_ref[0, 0], 0)[:, 0:NSA_DK].astype(BF16)
    vcmpT_ref[0, 0] = mlp(vc_ref[0, 0], 1).T[0:NSA_DK, :].astype(BF16)


def _nsa_cmp(kc, vc, w1ab, pe2, w2p):
    B, G, n, width = kc.shape
    hid = w1ab.shape[2] // 2
    blk = pl.BlockSpec((1, 1, n, width), lambda b, g: (b, g, 0, 0))
    return pl.pallas_call(
        functools.partial(_nsa_cmp_kernel, hid=hid),
        grid=(B, G),
        in_specs=[blk, blk, _const_spec(w1ab.shape), _const_spec(pe2.shape), _const_spec(w2p.shape)],
        out_specs=[pl.BlockSpec((1, 1, n, NSA_DK), lambda b, g: (b, g, 0, 0)),
                   pl.BlockSpec((1, 1, NSA_DK, n), lambda b, g: (b, g, 0, 0))],
        out_shape=[jax.ShapeDtypeStruct((B, G, n, NSA_DK), BF16), jax.ShapeDtypeStruct((B, G, NSA_DK, n), BF16)],
        compiler_params=_params(("arbitrary", "arbitrary")),
        name="nsa_cmp",
    )(kc, vc, w1ab, pe2, w2p)


def _nsa_attn_kernel(qc_ref, qr_ref, kcmp_ref, vcmpT_ref, ks_ref, vsT_ref, kw_ref, vwT_ref, gate_ref, o_ref,
                     ksa_sc, qaug_sc, qrT_sc, m_sc, l_sc, acc_sc, ocmp_sc, osel_sc, *, tq, seq):
    i = pl.program_id(2)
    H = NSA_HPG
    W = H * tq
    tk = tq
    nsel = seq // SEL_BLOCK
    ncmp = kcmp_ref.shape[2]

    @pl.when(i == 0)
    def _():
        ksa_sc[:, 0:NSA_DK] = ks_ref[0, 0]
        kb = lax.broadcasted_iota(jnp.int32, (seq, LANES - NSA_DK), 0) // SEL_BLOCK
        nn = lax.broadcasted_iota(jnp.int32, (seq, LANES - NSA_DK), 1)
        ksa_sc[:, NSA_DK:LANES] = jnp.where(kb == nn, 1.0, 0.0).astype(BF16)

    def heads_on_lanes(q_tile):
        qT = q_tile.astype(F32).T
        return jnp.concatenate([qT[h * NSA_DK:(h + 1) * NSA_DK, :] for h in range(H)], axis=1)

    qcT = heads_on_lanes(qc_ref[...]).astype(BF16)
    qrT = heads_on_lanes(qr_ref[...]).astype(BF16)
    qrT_sc[...] = qrT

    sc = _dot(kcmp_ref[0, 0], qcT)
    cidx = lax.broadcasted_iota(jnp.int32, (ncmp, W), 0)
    qpos = i * tq + (lax.broadcasted_iota(jnp.int32, (ncmp, W), 1) & (tq - 1))
    sc = jnp.where(cidx * CMP_STRIDE + (CMP_BLOCK - 1) <= qpos, sc, NEG)
    p = jnp.exp(sc - jnp.max(sc, axis=0, keepdims=True))
    p = p * (1.0 / jnp.sum(p, axis=0, keepdims=True))
    p = p * jnp.where(qpos[0:1, :] >= CMP_BLOCK - 1, 1.0, 0.0)
    ocmp_sc[...] = _dot(vcmpT_ref[0, 0], p.astype(BF16))

    psum = p[:, 0:tq]
    for h in range(1, H):
        psum = psum + p[:, h * tq:(h + 1) * tq]
    nn = lax.broadcasted_iota(jnp.int32, (nsel, ncmp), 0) * SEL_BLOCK
    cc = lax.broadcasted_iota(jnp.int32, (nsel, ncmp), 1) * CMP_STRIDE
    ov = jnp.maximum(jnp.minimum(cc + CMP_BLOCK, nn + SEL_BLOCK) - jnp.maximum(cc, nn), 0)
    ov = (ov.astype(F32) * (1.0 / CMP_STRIDE)).astype(BF16)
    p_hi = psum.astype(BF16)
    r1 = psum - p_hi.astype(F32)
    p_mid = r1.astype(BF16)
    p_lo = (r1 - p_mid.astype(F32)).astype(BF16)
    imp = _dot(ov, p_hi) + _dot(ov, p_mid) + _dot(ov, p_lo)

    nidx = lax.broadcasted_iota(jnp.int32, (nsel, tq), 0)
    qblk = (i * tq + lax.broadcasted_iota(jnp.int32, (nsel, tq), 1)) // SEL_BLOCK
    dist = qblk - nidx
    forced = (nidx == 0) | ((dist >= 0) & (dist < SEL_LOCAL))
    imp = jnp.where(forced, SEL_FORCE, jnp.where(dist >= 0, imp, -1.0))
    rank = jnp.zeros((nsel, tq), jnp.int32)
    for m in range(nsel):
        rm = imp[m:m + 1, :]
        before = (rm > imp) | ((rm == imp) & (nidx > m))
        rank = rank + before.astype(jnp.int32)
    sel = (rank < SEL_TOPK) & (dist >= 0)
    bias = jnp.where(sel, 0.0, NEG).astype(BF16)
    bias = jnp.concatenate([bias] * H, axis=1)
    qaug_sc[...] = jnp.concatenate(
        [qrT, bias, jnp.zeros((LANES - NSA_DK - nsel, W), BF16)], axis=0)

    t_loc = lax.broadcasted_iota(jnp.int32, (tk, W), 0)
    q_loc = lax.broadcasted_iota(jnp.int32, (tk, W), 1) & (tq - 1)

    def first_tile(s, vT):
        m = jnp.max(s, axis=0, keepdims=True)
        pt = jnp.exp(s - m)
        m_sc[...] = m
        l_sc[...] = jnp.sum(pt, axis=0, keepdims=True)
        acc_sc[...] = _dot(vT, pt.astype(BF16))

    def next_tile(s, vT):
        m_old = m_sc[...]
        m = jnp.maximum(m_old, jnp.max(s, axis=0, keepdims=True))
        alpha = jnp.exp(m_old - m)
        pt = jnp.exp(s - m)
        m_sc[...] = m
        l_sc[...] = alpha * l_sc[...] + jnp.sum(pt, axis=0, keepdims=True)
        acc_sc[...] = alpha * acc_sc[...] + _dot(vT, pt.astype(BF16))

    def key_rows(ref2d, j):
        return ref2d[pl.ds(pl.multiple_of(j * tk, tk), tk), :]

    first_tile(jnp.where(t_loc <= q_loc, _dot(key_rows(ksa_sc, i), qaug_sc[...]), NEG), vsT_ref[0, 0, i])

    def sel_body(j, carry):
        next_tile(_dot(key_rows(ksa_sc, j), qaug_sc[...]), vsT_ref[0, 0, j])
        return carry

    lax.fori_loop(0, i, sel_body, 0)
    osel_sc[...] = acc_sc[...] * (1.0 / l_sc[...])

    kw2 = kw_ref.at[0, 0]
    first_tile(jnp.where(t_loc <= q_loc, _dot(key_rows(kw2, i), qrT_sc[...]), NEG), vwT_ref[0, 0, i])
    span = WINDOW // tk

    def win_body(j, carry):
        next_tile(_dot(key_rows(kw2, j), qrT_sc[...]), vwT_ref[0, 0, j])
        return carry

    lax.fori_loop(jnp.maximum(i - span + 1, 0), i, win_body, 0)

    @pl.when(i >= span)
    def _():
        j = i - span
        next_tile(jnp.where(t_loc > q_loc, _dot(key_rows(kw2, j), qrT_sc[...]), NEG), vwT_ref[0, 0, j])

    owin = acc_sc[...] * (1.0 / l_sc[...])

    gate = gate_ref[0]
    ocmp = ocmp_sc[...]
    osel = osel_sc[...]
    outs = []
    for h in range(H):
        ls = slice(h * tq, (h + 1) * tq)
        outs.append(gate[3 * h:3 * h + 1, :] * ocmp[:, ls] + gate[3 * h + 1:3 * h + 2, :] * osel[:, ls]
                    + gate[3 * h + 2:3 * h + 3, :] * owin[:, ls])
    o_ref[...] = jnp.concatenate(outs, axis=0).T.astype(BF16)


def _nsa_attn(qc, qr, kcmp, vcmpT, ks, vsT, kw, vwT, gate, *, batch, seq):
    T, hd = qc.shape
    G = NSA_GROUPS
    tq = ATT_TILE
    nq = seq // tq
    gw = NSA_HPG * NSA_DK
    W = NSA_HPG * tq
    ncmp = kcmp.shape[2]
    qspec = pl.BlockSpec((tq, gw), lambda b, g, i: (b * nq + i, g))
    rows4 = pl.BlockSpec((1, 1, seq, NSA_DK), lambda b, g, i: (b, g, 0, 0))
    cols5 = pl.BlockSpec((1, 1, seq // tq, NSA_DK, tq), lambda b, g, i: (b, g, 0, 0, 0))
    return pl.pallas_call(
        functools.partial(_nsa_attn_kernel, tq=tq, seq=seq),
        grid=(batch, G, nq),
        in_specs=[qspec, qspec,
                  pl.BlockSpec((1, 1, ncmp, NSA_DK), lambda b, g, i: (b, g, 0, 0)),
                  pl.BlockSpec((1, 1, NSA_DK, ncmp), lambda b, g, i: (b, g, 0, 0)),
                  rows4, cols5, rows4, cols5,
                  pl.BlockSpec((1, GATE_PAD, tq), lambda b, g, i: (g, 0, b * nq + i))],
        out_specs=qspec,
        out_shape=jax.ShapeDtypeStruct((T, hd), BF16),
        scratch_shapes=[pltpu.VMEM((seq, LANES), BF16), pltpu.VMEM((LANES, W), BF16), pltpu.VMEM((NSA_DK, W), BF16),
                        pltpu.VMEM((1, W), F32), pltpu.VMEM((1, W), F32), pltpu.VMEM((NSA_DK, W), F32),
                        pltpu.VMEM((NSA_DK, W), F32), pltpu.VMEM((NSA_DK, W), F32)],
        compiler_params=_params(("arbitrary", "arbitrary", "arbitrary")),
        name="nsa_attn",
    )(qc, qr, kcmp, vcmpT, ks, vsT, kw, vwT, gate)


def _nsa_out_kernel(o_ref, z_ref, wout_ref, x_ref, gf_ref, y_ref, *, final):
    z = z_ref[...].astype(F32)
    u = o_ref[...].astype(F32) * (z * _sigmoid(z))
    y = x_ref[...] + _dot(u.astype(BF16), wout_ref[...])
    y_ref[...] = _rmsnorm(y, gf_ref[...]) if final else y


def _nsa_out(o, z, wout, x2, gf, *, final):
    T, hd = o.shape
    D = x2.shape[1]
    tm = NSA_ROWS
    row = lambda w: pl.BlockSpec((tm, w), lambda t: (t, 0))
    return pl.pallas_call(
        functools.partial(_nsa_out_kernel, final=final),
        grid=(T // tm,),
        in_specs=[row(hd), row(hd), _const_spec((hd, D)), row(D), _const_spec((1, D))],
        out_specs=row(D),
        out_shape=jax.ShapeDtypeStruct((T, D), F32),
        input_output_aliases={3: 0},
        compiler_params=_params(("arbitrary",)),
        name="nsa_out",
    )(o, z, wout, x2, gf)


def _final_norm_kernel(x_ref, g_ref, y_ref):
    y_ref[...] = _rmsnorm(x_ref[...], g_ref[...])


def _final_norm(x2, gf):
    T, D = x2.shape
    tm = NSA_ROWS
    row = pl.BlockSpec((tm, D), lambda t: (t, 0))
    return pl.pallas_call(
        _final_norm_kernel, grid=(T // tm,), in_specs=[row, _const_spec((1, D))], out_specs=row,
        out_shape=jax.ShapeDtypeStruct((T, D), F32), compiler_params=_params(("arbitrary",)),
        name="final_norm",
    )(x2, gf)


def _block_diag_tiles(w):
    nblk = w.shape[0]
    per = LANES // ML_QKV_BLK
    w4 = w.reshape(nblk // per, per, ML_QKV_BLK, ML_QKV_BLK)
    eye = jnp.eye(per, dtype=w.dtype)
    return jnp.einsum('cnij,nm->cnjmi', w4, eye).reshape(nblk // per, LANES, LANES).astype(BF16)


def _mlstm_layer(x2, norm, w_in, conv_w, conv_b, w_q, w_k, w_v, w_if, b_if, ln_w, skip, w_out, *, batch, seq):
    inner = w_in.shape[1] // 2
    wif = jnp.zeros((3 * inner, LANES), F32).at[:, :2 * ML_HEADS].set(w_if).astype(BF16)
    bif = jnp.zeros((1, LANES), F32).at[0, :2 * ML_HEADS].set(b_if)
    q, k, kT, v, xc, z, gcol, grow = _ml_front(
        x2, norm[None, :], w_in.astype(BF16), conv_w, conv_b[None, :],
        _block_diag_tiles(w_q), _block_diag_tiles(w_k), _block_diag_tiles(w_v), wif, bif, seq=seq)
    hn = _ml_core(q, k, kT, v, gcol, grow, ln_w[None, :], batch=batch, seq=seq)
    return _ml_out(hn, xc, z, skip[None, :], w_out.astype(BF16), x2)


def _nsa_layer(x2, norm, w_in, b_gate, cmp_pe, cmp_w1, cmp_w2, w_out, rope_tabs, final_g, *, batch, seq, final):
    G, DK = NSA_GROUPS, NSA_DK
    hd = NSA_HEADS * DK
    base = 2 * hd + 6 * G * DK
    D = w_in.shape[0]
    src = jnp.arange(3 * NSA_HEADS)
    dst = (src // (3 * NSA_HPG)) * GATE_PAD + src % (3 * NSA_HPG)
    wg = jnp.zeros((D, LANES), F32).at[:, dst].set(w_in[:, base:])
    bg = jnp.zeros((1, LANES), F32).at[0, dst].set(b_gate)
    w = jnp.concatenate([w_in[:, :base], wg], axis=1).astype(BF16)
    qc, qr, kc, vc, ks, vsT, kw, vwT, z, gate = _nsa_front(x2, norm[None, :], w, bg, *rope_tabs,
                                                           batch=batch, seq=seq)
    half = CMP_STRIDE * DK
    n_rows = seq // CMP_STRIDE
    w1ab = jnp.concatenate([cmp_w1[:, :half, :], cmp_w1[:, half:, :]], axis=2).astype(BF16)
    pe_flat = cmp_pe.reshape(2, 2, 1, half)
    pe2 = jnp.broadcast_to(pe_flat, (2, 2, 8, half)).reshape(2, 16, half).astype(BF16)
    w2p = jnp.zeros((2, cmp_w2.shape[1], LANES), F32).at[:, :, :DK].set(cmp_w2).astype(BF16)
    kcmp, vcmpT = _nsa_cmp(kc.reshape(batch, G, n_rows, half), vc.reshape(batch, G, n_rows, half), w1ab, pe2, w2p)
    o = _nsa_attn(qc, qr, kcmp, vcmpT, ks, vsT, kw, vwT, gate, batch=batch, seq=seq)
    return _nsa_out(o, z, w_out.astype(BF16), x2, final_g[None, :], final=final)


def _rope_tables(seq):
    half = ROPE_DIM // 2
    pos = jnp.arange(seq, dtype=F32)
    inv_freq = ROPE_THETA ** (-jnp.arange(0, ROPE_DIM, 2, dtype=F32) / ROPE_DIM)
    ang = pos[:, None] * inv_freq[None, :]
    cos, sin = jnp.cos(ang), jnp.sin(ang)
    lane = jnp.arange(LANES) % NSA_DK
    idx = lane % half
    ctab = jnp.where(lane[None, :] < ROPE_DIM, cos[:, idx], 1.0)
    s1 = jnp.where((lane[None, :] >= half) & (lane[None, :] < ROPE_DIM), sin[:, idx], 0.0)
    s2 = jnp.where(lane[None, :] < half, -sin[:, idx], 0.0)
    return ctab, s1, s2


def kernel(x, ml_norm, ml_w_in, ml_conv_w, ml_conv_b, ml_w_q, ml_w_k, ml_w_v, ml_w_if, ml_b_if, ml_ln_w, ml_skip,
           ml_w_out, nsa_norm, nsa_w_in, nsa_b_gate, nsa_cmp_pe, nsa_cmp_w1, nsa_cmp_w2, nsa_w_out, final_norm):
    batch, seq, d_model = x.shape
    depth = ml_norm.shape[0] + nsa_norm.shape[0]
    assert seq % ML_CHUNK == 0 and seq % NSA_ROWS == 0 and seq % ML_ROWS == 0 and WINDOW % ATT_TILE == 0
    x2 = x.reshape(batch * seq, d_model)
    rope_tabs = _rope_tables(seq)
    for i in range(depth):
        j = i // 2
        if i % 2 == 0:
            x2 = _mlstm_layer(x2, ml_norm[j], ml_w_in[j], ml_conv_w[j], ml_conv_b[j], ml_w_q[j], ml_w_k[j],
                              ml_w_v[j], ml_w_if[j], ml_b_if[j], ml_ln_w[j], ml_skip[j], ml_w_out[j],
                              batch=batch, seq=seq)
        else:
            x2 = _nsa_layer(x2, nsa_norm[j], nsa_w_in[j], nsa_b_gate[j], nsa_cmp_pe[j], nsa_cmp_w1[j],
                            nsa_cmp_w2[j], nsa_w_out[j], rope_tabs, final_norm, batch=batch, seq=seq,
                            final=(i == depth - 1))
    if depth % 2 == 1:
        x2 = _final_norm(x2, final_norm[None, :])
    return x2.reshape(batch, seq, d_model)
```

```python
import functools

import jax
import jax.numpy as jnp
from jax import lax
from jax.experimental import pallas as pl
from jax.experimental.pallas import tpu as pltpu

F32 = jnp.float32
BF16 = jnp.bfloat16

RMS_EPS = 1e-6
LN_EPS = 1e-6
NEG = -1e30

ML_HEADS = 4
ML_CONV = 4
ML_QKV_BLK = 4
ML_CHUNK = 256
ML_ROWS = 256

NSA_HEADS = 16
NSA_GROUPS = 4
NSA_HPG = NSA_HEADS // NSA_GROUPS
NSA_DK = 64
ROPE_DIM = NSA_DK // 4
ROPE_THETA = 500000.0
CMP_BLOCK = 32
CMP_STRIDE = 16
SEL_BLOCK = 64
SEL_TOPK = 16
SEL_LOCAL = 2
SEL_FORCE = 1e9
WINDOW = 512
NSA_ROWS = 256
ATT_TILE = 256
GATE_PAD = 16
V_ROWS = NSA_DK + 16
LOG2E = 1.4426950408889634

LANES = 128
VMEM_LIMIT = 56 * 1024 * 1024


def _dot(a, b):
    return jnp.dot(a, b, preferred_element_type=F32)


def _sigmoid(v):
    return 1.0 / (1.0 + jnp.exp(-v))


def _const_spec(shape):
    zeros = (0,) * len(shape)
    return pl.BlockSpec(shape, lambda *_: zeros)


def _params(sem):
    return pltpu.CompilerParams(dimension_semantics=sem, vmem_limit_bytes=VMEM_LIMIT)


def _rmsnorm(x, g):
    ms = jnp.mean(x * x, axis=-1, keepdims=True)
    return (x * lax.rsqrt(ms + RMS_EPS)) * g


def _ml_front_kernel(x_ref, g_ref, win_ref, convw_ref, convb_ref, wq_ref, wk_ref, wv_ref, wif_ref, bif_ref,
                     q_ref, k_ref, kT_ref, v_ref, xc_ref, z_ref, gcol_ref, grow_ref, xbuf,
                     *, tm, tiles_per_seq, inner, q_scale):
    t = pl.program_id(0)
    h = _rmsnorm(x_ref[...], g_ref[...])
    xz = _dot(h.astype(BF16), win_ref[...])
    x_in = xz[:, :inner]
    z_ref[...] = xz[:, inner:].astype(BF16)

    @pl.when(t % tiles_per_seq == 0)
    def _():
        xbuf[0:8, :] = jnp.zeros((8, inner), F32)

    xbuf[8:8 + tm, :] = x_in
    conv = jnp.broadcast_to(convb_ref[...], (tm, inner))
    for tap in range(ML_CONV):
        lo = 8 - (ML_CONV - 1) + tap
        conv = conv + xbuf[lo:lo + tm, :] * convw_ref[tap:tap + 1, :]
    xbuf[0:8, :] = xbuf[tm:tm + 8, :]
    xc = conv * _sigmoid(conv)
    xc_b = xc.astype(BF16)
    xc_ref[...] = xc_b
    xin_b = x_in.astype(BF16)

    gates = jnp.broadcast_to(bif_ref[...], (tm, LANES))
    for c in range(inner // LANES):
        sl = slice(c * LANES, (c + 1) * LANES)
        qc = _dot(xc_b[:, sl], wq_ref[c])
        kc = _dot(xc_b[:, sl], wk_ref[c])
        vc = _dot(xin_b[:, sl], wv_ref[c])
        kc_b = kc.astype(BF16)
        vc_b = vc.astype(BF16)
        q_ref[:, sl] = (qc * q_scale).astype(BF16)
        k_ref[:, sl] = kc_b
        kT_ref[sl, :] = kc.T.astype(BF16)
        v_ref[:, sl] = vc_b
        gates = gates + _dot(qc.astype(BF16), wif_ref[sl, :])
        gates = gates + _dot(kc_b, wif_ref[inner + c * LANES:inner + (c + 1) * LANES, :])
        gates = gates + _dot(vc_b, wif_ref[2 * inner + c * LANES:2 * inner + (c + 1) * LANES, :])

    logsig = -(jnp.maximum(-gates, 0.0) + jnp.log1p(jnp.exp(-jnp.abs(gates))))
    lane = lax.broadcasted_iota(jnp.int32, (tm, LANES), 1)
    gsel = jnp.where(lane < ML_HEADS, gates, logsig)
    gcol_ref[...] = gsel[:, 0:2 * ML_HEADS]
    grow_ref[...] = gsel.T[0:2 * ML_HEADS, :]


def _ml_front(x2, g, win, convw, convb, wq, wk, wv, wif, bif, *, seq):
    T, D = x2.shape
    inner = win.shape[1] // 2
    tm = ML_ROWS
    nq = inner // LANES
    kern = functools.partial(_ml_front_kernel, tm=tm, tiles_per_seq=seq // tm, inner=inner,
                             q_scale=float((inner // ML_HEADS) ** -0.5))
    row = lambda w: pl.BlockSpec((tm, w), lambda t: (t, 0))
    return pl.pallas_call(
        kern,
        grid=(T // tm,),
        in_specs=[row(D), _const_spec((1, D)), _const_spec((D, 2 * inner)),
                  _const_spec((ML_CONV, inner)), _const_spec((1, inner)),
                  _const_spec((nq, LANES, LANES)), _const_spec((nq, LANES, LANES)), _const_spec((nq, LANES, LANES)),
                  _const_spec((3 * inner, LANES)), _const_spec((1, LANES))],
        out_specs=[row(inner), row(inner), pl.BlockSpec((inner, tm), lambda t: (0, t)), row(inner),
                   row(inner), row(inner), row(2 * ML_HEADS),
                   pl.BlockSpec((2 * ML_HEADS, tm), lambda t: (0, t))],
        out_shape=[jax.ShapeDtypeStruct((T, inner), BF16), jax.ShapeDtypeStruct((T, inner), BF16),
                   jax.ShapeDtypeStruct((inner, T), BF16), jax.ShapeDtypeStruct((T, inner), BF16),
                   jax.ShapeDtypeStruct((T, inner), BF16), jax.ShapeDtypeStruct((T, inner), BF16),
                   jax.ShapeDtypeStruct((T, 2 * ML_HEADS), F32), jax.ShapeDtypeStruct((2 * ML_HEADS, T), F32)],
        scratch_shapes=[pltpu.VMEM((tm + 8, inner), F32)],
        compiler_params=_params(("arbitrary",)),
        name="ml_front",
    )(x2, g, win, convw, convb, wq, wk, wv, wif, bif)


def _ml_core_kernel(q_ref, k_ref, kT_ref, v_ref, gcol_ref, grow_ref, lnw_ref, o_ref, s_sc, n_sc, m_sc,
                    *, L, dh):
    c = pl.program_id(1)

    @pl.when(c == 0)
    def _():
        s_sc[...] = jnp.zeros_like(s_sc)
        n_sc[...] = jnp.zeros_like(n_sc)
        m_sc[...] = jnp.zeros_like(m_sc)

    gcol = gcol_ref[...]
    grow = grow_ref[...]
    row = lax.broadcasted_iota(jnp.int32, (L, L), 0)
    col = lax.broadcasted_iota(jnp.int32, (L, L), 1)
    causal = col <= row
    for h in range(ML_HEADS):
        sl = slice(h * dh, (h + 1) * dh)
        q = q_ref[:, sl]
        k = k_ref[:, sl]
        v = v_ref[:, sl]
        kT = kT_ref[sl, :]
        ic = gcol[:, h:h + 1]
        fc = gcol[:, ML_HEADS + h:ML_HEADS + h + 1]
        ir = grow[h:h + 1, :]
        fr = grow[ML_HEADS + h:ML_HEADS + h + 1, :]
        m_prev = m_sc[h][:, 0:1]
        b_col = jnp.sum(jnp.where(causal, fr, 0.0), axis=1, keepdims=True)
        b_row = jnp.sum(jnp.where(row <= col, fc, 0.0), axis=0, keepdims=True)
        dmat = jnp.where(causal, b_col - b_row + ir, -jnp.inf)
        inter = b_col + m_prev
        m_row = jnp.maximum(inter, jnp.max(dmat, axis=1, keepdims=True))
        w_intra = jnp.exp(dmat - m_row)
        w_inter = jnp.exp(inter - m_row)
        s = _dot(q, kT) * w_intra
        num = _dot(s.astype(BF16), v) + w_inter * _dot(q, s_sc[h].astype(BF16))
        den = jnp.sum(s, axis=1, keepdims=True) + w_inter * jnp.sum(q.astype(F32) * n_sc[h], axis=1, keepdims=True)
        hh = num * (1.0 / jnp.maximum(jnp.abs(den), jnp.exp(-m_row)))
        mu = jnp.mean(hh, axis=1, keepdims=True)
        cen = hh - mu
        var = jnp.mean(cen * cen, axis=1, keepdims=True)
        o_ref[:, sl] = ((cen * lax.rsqrt(var + LN_EPS)) * lnw_ref[:, sl]).astype(BF16)

        b_last = b_col[L - 1:L, :]
        a_row = b_last - b_row + ir
        a_col = b_last - b_col + ic
        m_new = jnp.maximum(b_last + m_prev, jnp.max(a_row, axis=1, keepdims=True))
        w_row = jnp.exp(a_row - m_new)
        w_col = jnp.exp(a_col - m_new)
        decay = jnp.exp(b_last + m_prev - m_new)
        s_sc[h] = decay * s_sc[h] + _dot((kT.astype(F32) * w_row).astype(BF16), v)
        n_sc[h] = decay * n_sc[h] + jnp.sum(k.astype(F32) * w_col, axis=0, keepdims=True)
        m_sc[h] = jnp.broadcast_to(m_new, (1, LANES))


def _ml_core(q, k, kT, v, gcol, grow, lnw, *, batch, seq):
    T, inner = q.shape
    L = ML_CHUNK
    nc = seq // L
    dh = inner // ML_HEADS
    kern = functools.partial(_ml_core_kernel, L=L, dh=dh)
    row = lambda w: pl.BlockSpec((L, w), lambda b, c: (b * nc + c, 0))
    colb = lambda r: pl.BlockSpec((r, L), lambda b, c: (0, b * nc + c))
    return pl.pallas_call(
        kern,
        grid=(batch, nc),
        in_specs=[row(inner), row(inner), colb(inner), row(inner), row(2 * ML_HEADS), colb(2 * ML_HEADS),
                  _const_spec((1, inner))],
        out_specs=row(inner),
        out_shape=jax.ShapeDtypeStruct((T, inner), BF16),
        scratch_shapes=[pltpu.VMEM((ML_HEADS, dh, dh), F32), pltpu.VMEM((ML_HEADS, 1, dh), F32),
                        pltpu.VMEM((ML_HEADS, 1, LANES), F32)],
        compiler_params=_params(("arbitrary", "arbitrary")),
        name="ml_core",
    )(q, k, kT, v, gcol, grow, lnw)


def _ml_out_kernel(hn_ref, xc_ref, z_ref, skip_ref, wout_ref, x_ref, o_ref):
    z = z_ref[...].astype(F32)
    u = (hn_ref[...].astype(F32) + skip_ref[...] * xc_ref[...].astype(F32)) * (z * _sigmoid(z))
    o_ref[...] = x_ref[...] + _dot(u.astype(BF16), wout_ref[...])


def _ml_out(hn, xc, z, skip, wout, x2):
    T, inner = hn.shape
    D = x2.shape[1]
    tm = ML_ROWS
    row = lambda w: pl.BlockSpec((tm, w), lambda t: (t, 0))
    return pl.pallas_call(
        _ml_out_kernel,
        grid=(T // tm,),
        in_specs=[row(inner), row(inner), row(inner), _const_spec((1, inner)), _const_spec((inner, D)), row(D)],
        out_specs=row(D),
        out_shape=jax.ShapeDtypeStruct((T, D), F32),
        input_output_aliases={5: 0},
        compiler_params=_params(("arbitrary",)),
        name="ml_out",
    )(hn, xc, z, skip, wout, x2)


def _rope(a, cos, s1, s2):
    return a * cos + pltpu.roll(a, ROPE_DIM // 2, 1) * s1 + pltpu.roll(a, LANES - ROPE_DIM // 2, 1) * s2


def _nsa_front_kernel(x_ref, g_ref, w_ref, bg_ref, cos_ref, s1_ref, s2_ref,
                      qc_ref, qr_ref, kc_ref, vc_ref, ks_ref, vsT_ref, kw_ref, vwT_ref, z_ref, gate_ref,
                      *, tm, hd, gd):
    h = _rmsnorm(x_ref[...], g_ref[...])
    xz = _dot(h.astype(BF16), w_ref[...])
    cos, s1, s2 = cos_ref[...], s1_ref[...], s2_ref[...]
    scale = NSA_DK ** -0.5 * LOG2E
    G = NSA_GROUPS
    for c in range(hd // LANES):
        sl = slice(c * LANES, (c + 1) * LANES)
        a = xz[:, sl]
        qc_ref[:, sl] = (a * scale).astype(BF16)
        qr_ref[:, sl] = (_rope(a, cos, s1, s2) * scale).astype(BF16)

    def group_slabs(off, rope):
        slabs = [xz[:, off + c * LANES:off + (c + 1) * LANES] for c in range(gd // LANES)]
        if rope:
            slabs = [_rope(a, cos, s1, s2) for a in slabs]
        return slabs

    def store_rows(ref, slabs):
        for g in range(G):
            a = slabs[(g * NSA_DK) // LANES]
            lo = (g * NSA_DK) % LANES
            ref[0, g] = a[:, lo:lo + NSA_DK].astype(BF16)

    def store_cols(ref, slabs):
        ones = jnp.ones((V_ROWS - NSA_DK, ATT_TILE), BF16)
        for g in range(G):
            aT = slabs[(g * NSA_DK) // LANES].T
            lo = (g * NSA_DK) % LANES
            for j in range(tm // ATT_TILE):
                ref[0, g, j, 0:NSA_DK, :] = aT[lo:lo + NSA_DK, j * ATT_TILE:(j + 1) * ATT_TILE].astype(BF16)
                ref[0, g, j, NSA_DK:V_ROWS, :] = ones

    store_rows(kc_ref, group_slabs(hd, False))
    store_rows(vc_ref, group_slabs(hd + gd, False))
    store_rows(ks_ref, group_slabs(hd + 2 * gd, True))
    store_cols(vsT_ref, group_slabs(hd + 3 * gd, False))
    store_rows(kw_ref, group_slabs(hd + 4 * gd, True))
    store_cols(vwT_ref, group_slabs(hd + 5 * gd, False))
    zoff = hd + 6 * gd
    z_ref[...] = xz[:, zoff:zoff + hd].astype(BF16)
    gl = xz[:, zoff + hd:zoff + hd + LANES] + bg_ref[...]
    gT = _sigmoid(gl).T
    for g in range(G):
        gate_ref[g] = gT[g * GATE_PAD:(g + 1) * GATE_PAD, :]


def _nsa_front(x2, g, w, bg, cos, s1, s2, *, batch, seq):
    T, D = x2.shape
    tm = NSA_ROWS
    tps = seq // tm
    G = NSA_GROUPS
    hd = NSA_HEADS * NSA_DK
    gd = G * NSA_DK
    ntk = seq // ATT_TILE
    kern = functools.partial(_nsa_front_kernel, tm=tm, hd=hd, gd=gd)
    row = lambda w_: pl.BlockSpec((tm, w_), lambda t: (t, 0))
    tab = pl.BlockSpec((tm, LANES), lambda t: (t % tps, 0))
    rows4 = pl.BlockSpec((1, G, tm, NSA_DK), lambda t: (t // tps, 0, t % tps, 0))
    cols5 = pl.BlockSpec((1, G, tm // ATT_TILE, V_ROWS, ATT_TILE), lambda t: (t // tps, 0, t % tps, 0, 0))
    rows_shape = jax.ShapeDtypeStruct((batch, G, seq, NSA_DK), BF16)
    cols_shape = jax.ShapeDtypeStruct((batch, G, ntk, V_ROWS, ATT_TILE), BF16)
    return pl.pallas_call(
        kern,
        grid=(T // tm,),
        in_specs=[row(D), _const_spec((1, D)), _const_spec(w.shape), _const_spec((1, LANES)), tab, tab, tab],
        out_specs=[row(hd), row(hd), rows4, rows4, rows4, cols5, rows4, cols5, row(hd),
                   pl.BlockSpec((G, GATE_PAD, tm), lambda t: (0, 0, t))],
        out_shape=[jax.ShapeDtypeStruct((T, hd), BF16), jax.ShapeDtypeStruct((T, hd), BF16),
                   rows_shape, rows_shape, rows_shape, cols_shape, rows_shape, cols_shape,
                   jax.ShapeDtypeStruct((T, hd), BF16), jax.ShapeDtypeStruct((G, GATE_PAD, T), F32)],
        compiler_params=_params(("arbitrary",)),
        name="nsa_front",
    )(x2, g, w, bg, cos, s1, s2)


def _nsa_cmp_kernel(kc_ref, vc_ref, w1_ref, pe_ref, w2_ref, kcmp_ref, vcmpT_ref, *, hid):
    def mlp(x, i):
        ab = _dot(x, w1_ref[i])
        pb = _dot(pe_ref[i], w1_ref[i])
        bias = pb[0:1, 0:hid] + pb[8:9, hid:2 * hid]
        h1 = ab[:, 0:hid] + pltpu.roll(ab[:, hid:2 * hid], ab.shape[0] - 1, 0) + bias
        return _dot((h1 * _sigmoid(h1)).astype(BF16), w2_ref[i])

    kcmp_ref[0, 0] = mlp(kc_ref[0, 0], 0)[:, 0:NSA_DK].astype(BF16)
    vcmpT_ref[0, 0] = mlp(vc_ref[0, 0], 1).T[0:NSA_DK, :].astype(BF16)


def _nsa_cmp(kc, vc, w1ab, pe2, w2p):
    B, G, n, width = kc.shape
    hid = w1ab.shape[2] // 2
    blk = pl.BlockSpec((1, 1, n, width), lambda b, g: (b, g, 0, 0))
    return pl.pallas_call(
        functools.partial(_nsa_cmp_kernel, hid=hid),
        grid=(B, G),
        in_specs=[blk, blk, _const_spec(w1ab.shape), _const_spec(pe2.shape), _const_spec(w2p.shape)],
        out_specs=[pl.BlockSpec((1, 1, n, NSA_DK), lambda b, g: (b, g, 0, 0)),
                   pl.BlockSpec((1, 1, NSA_DK, n), lambda b, g: (b, g, 0, 0))],
        out_shape=[jax.ShapeDtypeStruct((B, G, n, NSA_DK), BF16), jax.ShapeDtypeStruct((B, G, NSA_DK, n), BF16)],
        compiler_params=_params(("arbitrary", "arbitrary")),
        name="nsa_cmp",
    )(kc, vc, w1ab, pe2, w2p)


def _nsa_attn_kernel(qc_ref, qr_ref, kcmp_ref, vcmpT_ref, ks_ref, vsT_ref, kw_ref, vwT_ref, gate_ref, o_ref,
                     ksa_sc, qaug_sc, s_a, s_b, p_a, p_b, sw_sc, m_sc, alpha_sc, acc_sc, *, tq, seq):
    i = pl.program_id(2)
    H = NSA_HPG
    W = H * tq
    tk = tq
    nsel = seq // SEL_BLOCK
    ncmp = kcmp_ref.shape[2]
    nwin = WINDOW // tk + 1

    @pl.when(i == 0)
    def _():
        ksa_sc[:, 0:NSA_DK] = ks_ref[0, 0]
        kb = lax.broadcasted_iota(jnp.int32, (seq, LANES - NSA_DK), 0) // SEL_BLOCK
        nn = lax.broadcasted_iota(jnp.int32, (seq, LANES - NSA_DK), 1)
        ksa_sc[:, NSA_DK:LANES] = jnp.where(kb == nn, 1.0, 0.0).astype(BF16)

    def heads_on_lanes(q_tile):
        qT = q_tile.astype(F32).T
        return jnp.concatenate([qT[h * NSA_DK:(h + 1) * NSA_DK, :] for h in range(H)], axis=1)

    def key_rows(ref2d, j):
        return ref2d[pl.ds(pl.multiple_of(j * tk, tk), tk), :]

    qcT = heads_on_lanes(qc_ref[...]).astype(BF16)
    qrT = heads_on_lanes(qr_ref[...]).astype(BF16)
    delta = (lax.broadcasted_iota(jnp.int32, (tk, W), 0)
             - (lax.broadcasted_iota(jnp.int32, (tk, W), 1) & (tq - 1)))

    kw2 = kw_ref.at[0, 0]
    s_diag = _dot(key_rows(kw2, i), qrT)
    s_far = _dot(key_rows(kw2, jnp.maximum(i - (nwin - 1), 0)), qrT)
    sw_sc[0] = jnp.where(delta <= 0, s_diag, s_far)
    m_w = jnp.max(sw_sc[0], axis=0, keepdims=True)
    for d in range(1, nwin - 1):
        sw_sc[d] = _dot(key_rows(kw2, jnp.maximum(i - d, 0)), qrT)
        mt = jnp.max(sw_sc[d], axis=0, keepdims=True)
        m_w = jnp.maximum(m_w, jnp.where(i >= d, mt, NEG))
    pt = jnp.exp2(sw_sc[0] - m_w)
    far_thr = jnp.where(i >= nwin - 1, 0, 1 << 20)
    acc_w = _dot(vwT_ref[0, 0, i], jnp.where(delta <= 0, pt, 0.0).astype(BF16))
    acc_w = acc_w + _dot(vwT_ref[0, 0, jnp.maximum(i - (nwin - 1), 0)],
                         jnp.where(delta > far_thr, pt, 0.0).astype(BF16))
    for d in range(1, nwin - 1):
        pt = jnp.exp2(sw_sc[d] - (m_w + jnp.where(i >= d, 0.0, -NEG)))
        acc_w = acc_w + _dot(vwT_ref[0, 0, jnp.maximum(i - d, 0)], pt.astype(BF16))
    owin = acc_w[0:NSA_DK, :] * (1.0 / acc_w[V_ROWS - 1:V_ROWS, :])

    sc = _dot(kcmp_ref[0, 0], qcT)
    cidx = lax.broadcasted_iota(jnp.int32, (ncmp, W), 0)
    qpos = i * tq + (lax.broadcasted_iota(jnp.int32, (ncmp, W), 1) & (tq - 1))
    sc = jnp.where(cidx * CMP_STRIDE + (CMP_BLOCK - 1) <= qpos, sc, NEG)
    p = jnp.exp2(sc - jnp.max(sc, axis=0, keepdims=True))
    p = p * (1.0 / jnp.sum(p, axis=0, keepdims=True))
    p = p * jnp.where(qpos[0:1, :] >= CMP_BLOCK - 1, 1.0, 0.0)
    ocmp = _dot(vcmpT_ref[0, 0], p.astype(BF16))

    psum = p[:, 0:tq]
    for h in range(1, H):
        psum = psum + p[:, h * tq:(h + 1) * tq]
    nn = lax.broadcasted_iota(jnp.int32, (nsel, ncmp), 0) * SEL_BLOCK
    cc = lax.broadcasted_iota(jnp.int32, (nsel, ncmp), 1) * CMP_STRIDE
    ov = jnp.maximum(jnp.minimum(cc + CMP_BLOCK, nn + SEL_BLOCK) - jnp.maximum(cc, nn), 0)
    ov = (ov.astype(F32) * (1.0 / CMP_STRIDE)).astype(BF16)
    p_hi = psum.astype(BF16)
    r1 = psum - p_hi.astype(F32)
    p_mid = r1.astype(BF16)
    p_lo = (r1 - p_mid.astype(F32)).astype(BF16)
    imp = _dot(ov, p_hi) + _dot(ov, p_mid) + _dot(ov, p_lo)

    nidx = lax.broadcasted_iota(jnp.int32, (nsel, tq), 0)
    qblk = (i * tq + lax.broadcasted_iota(jnp.int32, (nsel, tq), 1)) // SEL_BLOCK
    dist = qblk - nidx
    forced = (nidx == 0) | ((dist >= 0) & (dist < SEL_LOCAL))
    imp = jnp.where(forced, SEL_FORCE, jnp.where(dist >= 0, imp, -1.0))
    rank = jnp.zeros((nsel, tq), jnp.int32)
    for m in range(nsel):
        rm = imp[m:m + 1, :]
        before = (rm > imp) | ((rm == imp) & (nidx > m))
        rank = rank + before.astype(jnp.int32)
    sel = (rank < SEL_TOPK) & (dist >= 0)
    bias = jnp.where(sel, 0.0, NEG).astype(BF16)
    bias = jnp.concatenate([bias] * H, axis=1)
    qaug = jnp.concatenate([qrT, bias, jnp.zeros((LANES - NSA_DK - nsel, W), BF16)], axis=0)
    qaug_sc[...] = qaug

    s = jnp.where(delta <= 0, _dot(key_rows(ksa_sc, i), qaug), NEG)
    m0 = jnp.max(s, axis=0, keepdims=True)
    m_sc[...] = m0
    alpha_sc[...] = jnp.ones((1, W), F32)
    acc_sc[...] = jnp.zeros((V_ROWS, W), F32)
    p_b[...] = jnp.exp2(s - m0).astype(BF16)
    s_a[...] = _dot(key_rows(ksa_sc, 0), qaug)

    def value_stage(jm, p_ref):
        acc_sc[...] = alpha_sc[...] * acc_sc[...] + _dot(vsT_ref[0, 0, jm], p_ref[...])

    def stages(j, s_cur, p_cur, s_nxt, p_prv):
        value_stage(jnp.where(j == 0, i, j - 1), p_prv)
        m_old = m_sc[...]
        m_new = jnp.maximum(m_old, jnp.max(s_cur[...], axis=0, keepdims=True))
        m_sc[...] = m_new
        alpha_sc[...] = jnp.exp2(m_old - m_new)
        p_cur[...] = jnp.exp2(s_cur[...] - m_new).astype(BF16)
        s_nxt[...] = _dot(key_rows(ksa_sc, jnp.minimum(j + 1, i - 1)), qaug_sc[...])

    def sel_body(j, carry):
        @pl.when((j & 1) == 0)
        def _():
            stages(j, s_a, p_a, s_b, p_b)

        @pl.when((j & 1) == 1)
        def _():
            stages(j, s_b, p_b, s_a, p_a)

        return carry

    lax.fori_loop(0, i, sel_body, 0)

    @pl.when((i & 1) == 0)
    def _():
        value_stage(jnp.where(i == 0, i, i - 1), p_b)

    @pl.when((i & 1) == 1)
    def _():
        value_stage(i - 1, p_a)

    acc = acc_sc[...]
    osel = acc[0:NSA_DK, :] * (1.0 / acc[V_ROWS - 1:V_ROWS, :])

    gate = gate_ref[0]
    outs = []
    for h in range(H):
        ls = slice(h * tq, (h + 1) * tq)
        outs.append(gate[3 * h:3 * h + 1, :] * ocmp[:, ls] + gate[3 * h + 1:3 * h + 2, :] * osel[:, ls]
                    + gate[3 * h + 2:3 * h + 3, :] * owin[:, ls])
    o_ref[...] = jnp.concatenate(outs, axis=0).T.astype(BF16)


def _nsa_attn(qc, qr, kcmp, vcmpT, ks, vsT, kw, vwT, gate, *, batch, seq):
    T, hd = qc.shape
    G = NSA_GROUPS
    tq = ATT_TILE
    nq = seq // tq
    gw = NSA_HPG * NSA_DK
    W = NSA_HPG * tq
    ncmp = kcmp.shape[2]
    nwin = WINDOW // tq + 1
    qspec = pl.BlockSpec((tq, gw), lambda b, g, i: (b * nq + i, g))
    rows4 = pl.BlockSpec((1, 1, seq, NSA_DK), lambda b, g, i: (b, g, 0, 0))
    cols5 = pl.BlockSpec((1, 1, seq // tq, V_ROWS, tq), lambda b, g, i: (b, g, 0, 0, 0))
    return pl.pallas_call(
        functools.partial(_nsa_attn_kernel, tq=tq, seq=seq),
        grid=(batch, G, nq),
        in_specs=[qspec, qspec,
                  pl.BlockSpec((1, 1, ncmp, NSA_DK), lambda b, g, i: (b, g, 0, 0)),
                  pl.BlockSpec((1, 1, NSA_DK, ncmp), lambda b, g, i: (b, g, 0, 0)),
                  rows4, cols5, rows4, cols5,
                  pl.BlockSpec((1, GATE_PAD, tq), lambda b, g, i: (g, 0, b * nq + i))],
        out_specs=qspec,
        out_shape=jax.ShapeDtypeStruct((T, hd), BF16),
        scratch_shapes=[pltpu.VMEM((seq, LANES), BF16), pltpu.VMEM((LANES, W), BF16),
                        pltpu.VMEM((tq, W), F32), pltpu.VMEM((tq, W), F32),
                        pltpu.VMEM((tq, W), BF16), pltpu.VMEM((tq, W), BF16), pltpu.VMEM((nwin - 1, tq, W), F32),
                        pltpu.VMEM((1, W), F32), pltpu.VMEM((1, W), F32), pltpu.VMEM((V_ROWS, W), F32)],
        compiler_params=_params(("arbitrary", "arbitrary", "arbitrary")),
        name="nsa_attn",
    )(qc, qr, kcmp, vcmpT, ks, vsT, kw, vwT, gate)


def _nsa_out_kernel(o_ref, z_ref, wout_ref, x_ref, gf_ref, y_ref, *, final):
    z = z_ref[...].astype(F32)
    u = o_ref[...].astype(F32) * (z * _sigmoid(z))
    y = x_ref[...] + _dot(u.astype(BF16), wout_ref[...])
    y_ref[...] = _rmsnorm(y, gf_ref[...]) if final else y


def _nsa_out(o, z, wout, x2, gf, *, final):
    T, hd = o.shape
    D = x2.shape[1]
    tm = NSA_ROWS
    row = lambda w: pl.BlockSpec((tm, w), lambda t: (t, 0))
    return pl.pallas_call(
        functools.partial(_nsa_out_kernel, final=final),
        grid=(T // tm,),
        in_specs=[row(hd), row(hd), _const_spec((hd, D)), row(D), _const_spec((1, D))],
        out_specs=row(D),
        out_shape=jax.ShapeDtypeStruct((T, D), F32),
        input_output_aliases={3: 0},
        compiler_params=_params(("arbitrary",)),
        name="nsa_out",
    )(o, z, wout, x2, gf)


def _final_norm_kernel(x_ref, g_ref, y_ref):
    y_ref[...] = _rmsnorm(x_ref[...], g_ref[...])


def _final_norm(x2, gf):
    T, D = x2.shape
    tm = NSA_ROWS
    row = pl.BlockSpec((tm, D), lambda t: (t, 0))
    return pl.pallas_call(
        _final_norm_kernel, grid=(T // tm,), in_specs=[row, _const_spec((1, D))], out_specs=row,
        out_shape=jax.ShapeDtypeStruct((T, D), F32), compiler_params=_params(("arbitrary",)),
        name="final_norm",
    )(x2, gf)


def _block_diag_tiles(w):
    nblk = w.shape[0]
    per = LANES // ML_QKV_BLK
    w4 = w.reshape(nblk // per, per, ML_QKV_BLK, ML_QKV_BLK)
    eye = jnp.eye(per, dtype=w.dtype)
    return jnp.einsum('cnij,nm->cnjmi', w4, eye).reshape(nblk // per, LANES, LANES).astype(BF16)


def _mlstm_layer(x2, norm, w_in, conv_w, conv_b, w_q, w_k, w_v, w_if, b_if, ln_w, skip, w_out, *, batch, seq):
    inner = w_in.shape[1] // 2
    wif = jnp.zeros((3 * inner, LANES), F32).at[:, :2 * ML_HEADS].set(w_if).astype(BF16)
    bif = jnp.zeros((1, LANES), F32).at[0, :2 * ML_HEADS].set(b_if)
    q, k, kT, v, xc, z, gcol, grow = _ml_front(
        x2, norm[None, :], w_in.astype(BF16), conv_w, conv_b[None, :],
        _block_diag_tiles(w_q), _block_diag_tiles(w_k), _block_diag_tiles(w_v), wif, bif, seq=seq)
    hn = _ml_core(q, k, kT, v, gcol, grow, ln_w[None, :], batch=batch, seq=seq)
    return _ml_out(hn, xc, z, skip[None, :], w_out.astype(BF16), x2)


def _nsa_layer(x2, norm, w_in, b_gate, cmp_pe, cmp_w1, cmp_w2, w_out, rope_tabs, final_g, *, batch, seq, final):
    G, DK = NSA_GROUPS, NSA_DK
    hd = NSA_HEADS * DK
    base = 2 * hd + 6 * G * DK
    D = w_in.shape[0]
    src = jnp.arange(3 * NSA_HEADS)
    dst = (src // (3 * NSA_HPG)) * GATE_PAD + src % (3 * NSA_HPG)
    wg = jnp.zeros((D, LANES), F32).at[:, dst].set(w_in[:, base:])
    bg = jnp.zeros((1, LANES), F32).at[0, dst].set(b_gate)
    w = jnp.concatenate([w_in[:, :base], wg], axis=1).astype(BF16)
    qc, qr, kc, vc, ks, vsT, kw, vwT, z, gate = _nsa_front(x2, norm[None, :], w, bg, *rope_tabs,
                                                           batch=batch, seq=seq)
    half = CMP_STRIDE * DK
    n_rows = seq // CMP_STRIDE
    w1ab = jnp.concatenate([cmp_w1[:, :half, :], cmp_w1[:, half:, :]], axis=2).astype(BF16)
    pe_flat = cmp_pe.reshape(2, 2, 1, half)
    pe2 = jnp.broadcast_to(pe_flat, (2, 2, 8, half)).reshape(2, 16, half).astype(BF16)
    w2p = jnp.zeros((2, cmp_w2.shape[1], LANES), F32).at[:, :, :DK].set(cmp_w2).astype(BF16)
    kcmp, vcmpT = _nsa_cmp(kc.reshape(batch, G, n_rows, half), vc.reshape(batch, G, n_rows, half), w1ab, pe2, w2p)
    o = _nsa_attn(qc, qr, kcmp, vcmpT, ks, vsT, kw, vwT, gate, batch=batch, seq=seq)
    return _nsa_out(o, z, w_out.astype(BF16), x2, final_g[None, :], final=final)


def _rope_tables(seq):
    half = ROPE_DIM // 2
    pos = jnp.arange(seq, dtype=F32)
    inv_freq = ROPE_THETA ** (-jnp.arange(0, ROPE_DIM, 2, dtype=F32) / ROPE_DIM)
    ang = pos[:, None] * inv_freq[None, :]
    cos, sin = jnp.cos(ang), jnp.sin(ang)
    lane = jnp.arange(LANES) % NSA_DK
    idx = lane % half
    ctab = jnp.where(lane[None, :] < ROPE_DIM, cos[:, idx], 1.0)
    s1 = jnp.where((lane[None, :] >= half) & (lane[None, :] < ROPE_DIM), sin[:, idx], 0.0)
    s2 = jnp.where(lane[None, :] < half, -sin[:, idx], 0.0)
    return ctab, s1, s2


def kernel(x, ml_norm, ml_w_in, ml_conv_w, ml_conv_b, ml_w_q, ml_w_k, ml_w_v, ml_w_if, ml_b_if, ml_ln_w, ml_skip,
           ml_w_out, nsa_norm, nsa_w_in, nsa_b_gate, nsa_cmp_pe, nsa_cmp_w1, nsa_cmp_w2, nsa_w_out, final_norm):
    batch, seq, d_model = x.shape
    depth = ml_norm.shape[0] + nsa_norm.shape[0]
    assert seq % ML_CHUNK == 0 and seq % NSA_ROWS == 0 and seq % ML_ROWS == 0 and WINDOW % ATT_TILE == 0
    x2 = x.reshape(batch * seq, d_model)
    rope_tabs = _rope_tables(seq)
    for i in range(depth):
        j = i // 2
        if i % 2 == 0:
            x2 = _mlstm_layer(x2, ml_norm[j], ml_w_in[j], ml_conv_w[j], ml_conv_b[j], ml_w_q[j], ml_w_k[j],
                              ml_w_v[j], ml_w_if[j], ml_b_if[j], ml_ln_w[j], ml_skip[j], ml_w_out[j],
                              batch=batch, seq=seq)
        else:
            x2 = _nsa_layer(x2, nsa_norm[j], nsa_w_in[j], nsa_b_gate[j], nsa_cmp_pe[j], nsa_cmp_w1[j],
                            nsa_cmp_w2[j], nsa_w_out[j], rope_tabs, final_norm, batch=batch, seq=seq,
                            final=(i == depth - 1))
    if depth % 2 == 1:
        x2 = _final_norm(x2, final_norm[None, :])
    return x2.reshape(batch, seq, d_model)
```

```python
import functools

import jax
import jax.numpy as jnp
from jax import lax
from jax.experimental import pallas as pl
from jax.experimental.pallas import tpu as pltpu

F32 = jnp.float32
BF16 = jnp.bfloat16

RMS_EPS = 1e-6
LN_EPS = 1e-6
NEG = -1e30

ML_HEADS = 4
ML_CONV = 4
ML_QKV_BLK = 4
ML_CHUNK = 256
ML_ROWS = 512

NSA_HEADS = 16
NSA_GROUPS = 4
NSA_HPG = NSA_HEADS // NSA_GROUPS
NSA_DK = 64
ROPE_DIM = NSA_DK // 4
ROPE_THETA = 500000.0
CMP_BLOCK = 32
CMP_STRIDE = 16
SEL_BLOCK = 64
SEL_TOPK = 16
SEL_LOCAL = 2
SEL_FORCE = 1e9
WINDOW = 512
NSA_ROWS = 512
ATT_TILE = 256
GATE_PAD = 16
V_ROWS = NSA_DK + 16
LOG2E = 1.4426950408889634

LANES = 128
MXU_TILE = 256
VMEM_LIMIT = 56 * 1024 * 1024


def _dot(a, b):
    return jnp.dot(a, b, preferred_element_type=F32)


def _sigmoid(v):
    return 1.0 / (1.0 + jnp.exp(-v))


def _const_spec(shape):
    zeros = (0,) * len(shape)
    return pl.BlockSpec(shape, lambda *_: zeros, pipeline_mode=pl.Buffered(1))


def _params(sem, flags=None):
    return pltpu.CompilerParams(dimension_semantics=sem, vmem_limit_bytes=VMEM_LIMIT, flags=flags)


def _rmsnorm(x, g):
    ms = jnp.mean(x * x, axis=-1, keepdims=True)
    return (x * lax.rsqrt(ms + RMS_EPS)) * g


def _ml_front_kernel(x_ref, g_ref, win_ref, convw_ref, convb_ref, wq_ref, wk_ref, wv_ref, gx_ref, gi_ref, bif_ref,
                     q_ref, kT_ref, v_ref, xc_ref, z_ref, gcol_ref, grow_ref, xbuf, h_sc,
                     *, tm, tiles_per_seq, inner, q_scale):
    t = pl.program_id(0)
    h_sc[...] = _rmsnorm(x_ref[...], g_ref[...]).astype(BF16)

    @pl.when(t % tiles_per_seq == 0)
    def _():
        xbuf[:, 0:8, :] = jnp.zeros((inner // MXU_TILE, 8, MXU_TILE), F32)

    nchunk = inner // MXU_TILE

    def in_proj(c):
        sl = slice(c * MXU_TILE, (c + 1) * MXU_TILE)
        xbuf[c, 8:8 + tm, :] = _dot(h_sc[...], win_ref[:, sl])
        z_ref[:, sl] = _dot(h_sc[...], win_ref[:, inner + c * MXU_TILE:inner + (c + 1) * MXU_TILE]).astype(BF16)

    in_proj(0)
    gates = jnp.broadcast_to(bif_ref[...], (tm, LANES))
    for c in range(nchunk):
        sl = slice(c * MXU_TILE, (c + 1) * MXU_TILE)
        if c + 1 < nchunk:
            in_proj(c + 1)
        conv = convb_ref[:, sl]
        for tap in range(ML_CONV):
            lo = 8 - (ML_CONV - 1) + tap
            conv = conv + xbuf[c, lo:lo + tm, :] * convw_ref[tap:tap + 1, sl]
        xin_b = xbuf[c, 8:8 + tm, :].astype(BF16)
        xbuf[c, 0:8, :] = xbuf[c, tm:tm + 8, :]
        xc_b = (conv * _sigmoid(conv)).astype(BF16)
        xc_ref[:, sl] = xc_b
        q_ref[:, sl] = (_dot(xc_b, wq_ref[c]) * q_scale).astype(BF16)
        kT_ref[sl, :] = _dot(xc_b, wk_ref[c]).T.astype(BF16)
        v_ref[:, sl] = _dot(xin_b, wv_ref[c]).astype(BF16)
        gates = gates + _dot(xc_b, gx_ref[sl, :]) + _dot(xin_b, gi_ref[sl, :])

    logsig = -(jnp.maximum(-gates, 0.0) + jnp.log1p(jnp.exp(-jnp.abs(gates))))
    lane = lax.broadcasted_iota(jnp.int32, (tm, LANES), 1)
    gsel = jnp.where(lane < ML_HEADS, gates, logsig)
    gcol_ref[...] = gsel[:, 0:2 * ML_HEADS]
    grow_ref[...] = gsel.T[0:2 * ML_HEADS, :]


def _ml_front(x2, g, win, convw, convb, wq, wk, wv, gx, gi, bif, *, seq):
    T, D = x2.shape
    inner = win.shape[1] // 2
    tm = ML_ROWS
    nq = inner // MXU_TILE
    kern = functools.partial(_ml_front_kernel, tm=tm, tiles_per_seq=seq // tm, inner=inner,
                             q_scale=float((inner // ML_HEADS) ** -0.5))
    row = lambda w: pl.BlockSpec((tm, w), lambda t: (t, 0))
    bd = _const_spec((nq, MXU_TILE, MXU_TILE))
    return pl.pallas_call(
        kern,
        grid=(T // tm,),
        in_specs=[row(D), _const_spec((1, D)), _const_spec((D, 2 * inner)),
                  _const_spec((ML_CONV, inner)), _const_spec((1, inner)), bd, bd, bd,
                  _const_spec((inner, LANES)), _const_spec((inner, LANES)), _const_spec((1, LANES))],
        out_specs=[row(inner), pl.BlockSpec((inner, tm), lambda t: (0, t)), row(inner),
                   row(inner), row(inner), row(2 * ML_HEADS),
                   pl.BlockSpec((2 * ML_HEADS, tm), lambda t: (0, t))],
        out_shape=[jax.ShapeDtypeStruct((T, inner), BF16),
                   jax.ShapeDtypeStruct((inner, T), BF16), jax.ShapeDtypeStruct((T, inner), BF16),
                   jax.ShapeDtypeStruct((T, inner), BF16), jax.ShapeDtypeStruct((T, inner), BF16),
                   jax.ShapeDtypeStruct((T, 2 * ML_HEADS), F32), jax.ShapeDtypeStruct((2 * ML_HEADS, T), F32)],
        scratch_shapes=[pltpu.VMEM((nq, tm + 8, MXU_TILE), F32), pltpu.VMEM((tm, D), BF16)],
        compiler_params=_params(("arbitrary",)),
        name="ml_front",
    )(x2, g, win, convw, convb, wq, wk, wv, gx, gi, bif)


def _ml_core_kernel(q_ref, kT_ref, v_ref, gcol_ref, grow_ref, lnw_ref, o_ref, s_sc, n_sc, m_sc,
                    qk_sc, qs_sc, qn_sc, *, L, dh):
    c = pl.program_id(1)
    ones = jnp.ones((L, LANES), BF16)

    @pl.when(c == 0)
    def _():
        s_sc[...] = jnp.zeros_like(s_sc)
        n_sc[...] = jnp.zeros_like(n_sc)
        m_sc[...] = jnp.zeros_like(m_sc)

    gcol = gcol_ref[...]
    grow = grow_ref[...]
    row = lax.broadcasted_iota(jnp.int32, (L, L), 0)
    col = lax.broadcasted_iota(jnp.int32, (L, L), 1)
    causal = col <= row
    def query_matmuls(h):
        sl = slice(h * dh, (h + 1) * dh)
        q = q_ref[:, sl]
        qk_sc[h % 2] = _dot(q, kT_ref[sl, :])
        qs_sc[h % 2] = _dot(q, s_sc[h].astype(BF16))
        qn_sc[h % 2] = _dot(q, n_sc[h].astype(BF16))

    query_matmuls(0)
    for h in range(ML_HEADS):
        if h + 1 < ML_HEADS:
            query_matmuls(h + 1)
        sl = slice(h * dh, (h + 1) * dh)
        v = v_ref[:, sl]
        kT = kT_ref[sl, :]
        fc = gcol[:, ML_HEADS + h:ML_HEADS + h + 1]
        ir = grow[h:h + 1, :]
        fr = grow[ML_HEADS + h:ML_HEADS + h + 1, :]
        m_prev = m_sc[h][:, 0:1]
        b_col = jnp.sum(jnp.where(causal, fr, 0.0), axis=1, keepdims=True)
        b_row = jnp.sum(jnp.where(row <= col, fc, 0.0), axis=0, keepdims=True)
        dmat = jnp.where(causal, b_col - b_row + ir, -jnp.inf)
        inter = b_col + m_prev
        m_row = jnp.maximum(inter, jnp.max(dmat, axis=1, keepdims=True))
        w_intra = jnp.exp(dmat - m_row)
        w_inter = jnp.exp(inter - m_row)
        s = qk_sc[h % 2] * w_intra
        num = _dot(s.astype(BF16), v) + w_inter * qs_sc[h % 2]
        den = jnp.sum(s, axis=1, keepdims=True) + w_inter * qn_sc[h % 2][:, 0:1]
        hh = num * (1.0 / jnp.maximum(jnp.abs(den), jnp.exp(-m_row)))
        mu = jnp.mean(hh, axis=1, keepdims=True)
        cen = hh - mu
        var = jnp.mean(cen * cen, axis=1, keepdims=True)
        o_ref[:, sl] = ((cen * lax.rsqrt(var + LN_EPS)) * lnw_ref[:, sl]).astype(BF16)

        b_last = b_col[L - 1:L, :]
        a_row = b_last - b_row + ir
        m_new = jnp.maximum(b_last + m_prev, jnp.max(a_row, axis=1, keepdims=True))
        decay = jnp.exp(b_last + m_prev - m_new)
        kw = kT * jnp.exp(a_row - m_new).astype(BF16)
        s_sc[h] = decay * s_sc[h] + _dot(kw, v)
        n_sc[h] = decay * n_sc[h] + _dot(kw, ones)
        m_sc[h] = jnp.broadcast_to(m_new, (1, LANES))


def _ml_core(q, kT, v, gcol, grow, lnw, *, batch, seq):
    T, inner = q.shape
    L = ML_CHUNK
    nc = seq // L
    dh = inner // ML_HEADS
    kern = functools.partial(_ml_core_kernel, L=L, dh=dh)
    row = lambda w: pl.BlockSpec((L, w), lambda b, c: (b * nc + c, 0))
    colb = lambda r: pl.BlockSpec((r, L), lambda b, c: (0, b * nc + c))
    return pl.pallas_call(
        kern,
        grid=(batch, nc),
        in_specs=[row(inner), colb(inner), row(inner), row(2 * ML_HEADS), colb(2 * ML_HEADS),
                  _const_spec((1, inner))],
        out_specs=row(inner),
        out_shape=jax.ShapeDtypeStruct((T, inner), BF16),
        scratch_shapes=[pltpu.VMEM((ML_HEADS, dh, dh), F32), pltpu.VMEM((ML_HEADS, dh, LANES), F32),
                        pltpu.VMEM((ML_HEADS, 1, LANES), F32), pltpu.VMEM((2, L, L), F32),
                        pltpu.VMEM((2, L, dh), F32), pltpu.VMEM((2, L, LANES), F32)],
        compiler_params=_params(("arbitrary", "arbitrary")),
        name="ml_core",
    )(q, kT, v, gcol, grow, lnw)


def _ml_out_kernel(hn_ref, xc_ref, z_ref, skip_ref, wout_ref, x_ref, o_ref):
    z = z_ref[...].astype(F32)
    u = (hn_ref[...].astype(F32) + skip_ref[...] * xc_ref[...].astype(F32)) * (z * _sigmoid(z))
    o_ref[...] = x_ref[...] + _dot(u.astype(BF16), wout_ref[...])


def _ml_out(hn, xc, z, skip, wout, x2, *, in_place):
    T, inner = hn.shape
    D = x2.shape[1]
    tm = ML_ROWS
    row = lambda w: pl.BlockSpec((tm, w), lambda t: (t, 0))
    return pl.pallas_call(
        _ml_out_kernel,
        grid=(T // tm,),
        in_specs=[row(inner), row(inner), row(inner), _const_spec((1, inner)), _const_spec((inner, D)), row(D)],
        out_specs=row(D),
        out_shape=jax.ShapeDtypeStruct((T, D), F32),
        input_output_aliases={5: 0} if in_place else {},
        compiler_params=_params(("arbitrary",)),
        name="ml_out",
    )(hn, xc, z, skip, wout, x2)


def _rope(a, cos, s1, s2):
    return a * cos + pltpu.roll(a, ROPE_DIM // 2, 1) * s1 + pltpu.roll(a, LANES - ROPE_DIM // 2, 1) * s2


def _nsa_front_kernel(x_ref, g_ref, w_ref, bg_ref, cos_ref, s1_ref, s2_ref,
                      qc_ref, qr_ref, kc_ref, vc_ref, ks_ref, vsT_ref, kw_ref, vwT_ref, z_ref, gate_ref,
                      h_sc, sec_sc, *, tm, hd, gd):
    h_sc[...] = _rmsnorm(x_ref[...], g_ref[...]).astype(BF16)
    cos, s1, s2 = cos_ref[...], s1_ref[...], s2_ref[...]
    scale = NSA_DK ** -0.5 * LOG2E
    G = NSA_GROUPS

    def slabs_of(sec, rope):
        slabs = [sec[:, c * LANES:(c + 1) * LANES] for c in range(sec.shape[1] // LANES)]
        return [_rope(a, cos, s1, s2) for a in slabs] if rope else slabs

    def store_q(c):
        def run(sec):
            for half, a in enumerate(slabs_of(sec, False)):
                sl = slice(c * MXU_TILE + half * LANES, c * MXU_TILE + (half + 1) * LANES)
                qc_ref[:, sl] = (a * scale).astype(BF16)
                qr_ref[:, sl] = (_rope(a, cos, s1, s2) * scale).astype(BF16)
        return run

    def store_rows(ref, rope):
        def run(sec):
            slabs = slabs_of(sec, rope)
            for g in range(G):
                lo = (g * NSA_DK) % LANES
                ref[0, g] = slabs[(g * NSA_DK) // LANES][:, lo:lo + NSA_DK].astype(ref.dtype)
        return run

    def store_cols(ref):
        def run(sec):
            slabs = slabs_of(sec, False)
            ones = jnp.ones((V_ROWS - NSA_DK, ATT_TILE), BF16)
            for g in range(G):
                aT = slabs[(g * NSA_DK) // LANES].T
                lo = (g * NSA_DK) % LANES
                for j in range(tm // ATT_TILE):
                    ref[0, g, j, 0:NSA_DK, :] = aT[lo:lo + NSA_DK, j * ATT_TILE:(j + 1) * ATT_TILE].astype(BF16)
                    ref[0, g, j, NSA_DK:V_ROWS, :] = ones
        return run

    def store_z(c):
        def run(sec):
            z_ref[:, c * MXU_TILE:(c + 1) * MXU_TILE] = sec.astype(BF16)
        return run

    def store_gate(sec):
        gT = _sigmoid(sec + bg_ref[...]).T
        for g in range(G):
            gate_ref[g] = gT[g * GATE_PAD:(g + 1) * GATE_PAD, :]

    zoff = hd + 6 * gd
    sections = ([(c * MXU_TILE, MXU_TILE, store_q(c)) for c in range(hd // MXU_TILE)]
                + [(hd, gd, store_rows(kc_ref, False)), (hd + gd, gd, store_rows(vc_ref, False)),
                   (hd + 2 * gd, gd, store_rows(ks_ref, True)), (hd + 3 * gd, gd, store_cols(vsT_ref)),
                   (hd + 4 * gd, gd, store_rows(kw_ref, True)), (hd + 5 * gd, gd, store_cols(vwT_ref))]
                + [(zoff + c * MXU_TILE, MXU_TILE, store_z(c)) for c in range(hd // MXU_TILE)]
                + [(zoff + hd, LANES, store_gate)])

    def project(k):
        off, width, _ = sections[k]
        sec_sc[k % 2, :, 0:width] = _dot(h_sc[...], w_ref[:, off:off + width])

    project(0)
    for k, (_, width, consume) in enumerate(sections):
        if k + 1 < len(sections):
            project(k + 1)
        consume(sec_sc[k % 2, :, 0:width])


def _nsa_front(x2, g, w, bg, cos, s1, s2, *, batch, seq):
    T, D = x2.shape
    tm = NSA_ROWS
    tps = seq // tm
    G = NSA_GROUPS
    hd = NSA_HEADS * NSA_DK
    gd = G * NSA_DK
    ntk = seq // ATT_TILE
    kern = functools.partial(_nsa_front_kernel, tm=tm, hd=hd, gd=gd)
    row = lambda w_: pl.BlockSpec((tm, w_), lambda t: (t, 0))
    tab = pl.BlockSpec((tm, LANES), lambda t: (t % tps, 0))
    rows4 = pl.BlockSpec((1, G, tm, NSA_DK), lambda t: (t // tps, 0, t % tps, 0))
    cols5 = pl.BlockSpec((1, G, tm // ATT_TILE, V_ROWS, ATT_TILE), lambda t: (t // tps, 0, t % tps, 0, 0))
    rows_shape = jax.ShapeDtypeStruct((batch, G, seq, NSA_DK), BF16)
    rows_f32 = jax.ShapeDtypeStruct((batch, G, seq, NSA_DK), F32)
    cols_shape = jax.ShapeDtypeStruct((batch, G, ntk, V_ROWS, ATT_TILE), BF16)
    return pl.pallas_call(
        kern,
        grid=(T // tm,),
        in_specs=[row(D), _const_spec((1, D)), _const_spec(w.shape), _const_spec((1, LANES)), tab, tab, tab],
        out_specs=[row(hd), row(hd), rows4, rows4, rows4, cols5, rows4, cols5, row(hd),
                   pl.BlockSpec((G, GATE_PAD, tm), lambda t: (0, 0, t))],
        out_shape=[jax.ShapeDtypeStruct((T, hd), BF16), jax.ShapeDtypeStruct((T, hd), BF16),
                   rows_f32, rows_f32, rows_shape, cols_shape, rows_shape, cols_shape,
                   jax.ShapeDtypeStruct((T, hd), BF16), jax.ShapeDtypeStruct((G, GATE_PAD, T), F32)],
        scratch_shapes=[pltpu.VMEM((tm, D), BF16), pltpu.VMEM((2, tm, MXU_TILE), F32)],
        compiler_params=_params(("arbitrary",)),
        name="nsa_front",
    )(x2, g, w, bg, cos, s1, s2)


def _nsa_cmp_kernel(kc_ref, vc_ref, w1_ref, pe_ref, w2_ref, kcmp_ref, vcmpT_ref, *, hid, n):
    def mlp(x_ref, i):
        ab = jnp.zeros((n, 2 * hid), F32)
        for r in range(CMP_STRIDE):
            xr = x_ref[0, 0, pl.ds(r, n, stride=CMP_STRIDE), :]
            ab = ab + _dot(xr.astype(BF16), w1_ref[i, r * NSA_DK:(r + 1) * NSA_DK, :])
        pb = _dot(pe_ref[i], w1_ref[i])
        bias = pb[0:1, 0:hid] + pb[8:9, hid:2 * hid]
        h1 = ab[:, 0:hid] + pltpu.roll(ab[:, hid:2 * hid], n - 1, 0) + bias
        return _dot((h1 * _sigmoid(h1)).astype(BF16), w2_ref[i])

    kcmp_ref[0, 0] = mlp(kc_ref, 0)[:, 0:NSA_DK].astype(BF16)
    vcmpT_ref[0, 0] = mlp(vc_ref, 1).T[0:NSA_DK, :].astype(BF16)


def _nsa_cmp(kc, vc, w1ab, pe2, w2p):
    B, G, seq, width = kc.shape
    n = seq // CMP_STRIDE
    hid = w1ab.shape[2] // 2
    blk = pl.BlockSpec((1, 1, seq, width), lambda b, g: (b, g, 0, 0))
    return pl.pallas_call(
        functools.partial(_nsa_cmp_kernel, hid=hid, n=n),
        grid=(B, G),
        in_specs=[blk, blk, _const_spec(w1ab.shape), _const_spec(pe2.shape), _const_spec(w2p.shape)],
        out_specs=[pl.BlockSpec((1, 1, n, NSA_DK), lambda b, g: (b, g, 0, 0)),
                   pl.BlockSpec((1, 1, NSA_DK, n), lambda b, g: (b, g, 0, 0))],
        out_shape=[jax.ShapeDtypeStruct((B, G, n, NSA_DK), BF16), jax.ShapeDtypeStruct((B, G, NSA_DK, n), BF16)],
        compiler_params=_params(("arbitrary", "arbitrary")),
        name="nsa_cmp",
    )(kc, vc, w1ab, pe2, w2p)


def _nsa_attn_kernel(qc_ref, qr_ref, kcmp_ref, vcmpT_ref, ks_ref, vsT_ref, kw_ref, vwT_ref, gate_ref, o_ref,
                     ksa_sc, qaug_sc, s_a, s_b, p_a, p_b, sw_sc, m_sc, alpha_sc, acc_sc, *, tq, seq):
    i = pl.program_id(2)
    H = NSA_HPG
    W = H * tq
    tk = tq
    nsel = seq // SEL_BLOCK
    ncmp = kcmp_ref.shape[2]
    nwin = WINDOW // tk + 1

    @pl.when(i == 0)
    def _():
        ksa_sc[:, 0:NSA_DK] = ks_ref[0, 0]
        kb = lax.broadcasted_iota(jnp.int32, (seq, LANES - NSA_DK), 0) // SEL_BLOCK
        nn = lax.broadcasted_iota(jnp.int32, (seq, LANES - NSA_DK), 1)
        ksa_sc[:, NSA_DK:LANES] = jnp.where(kb == nn, 1.0, 0.0).astype(BF16)

    def heads_on_lanes(q_tile):
        qT = q_tile.astype(F32).T
        return jnp.concatenate([qT[h * NSA_DK:(h + 1) * NSA_DK, :] for h in range(H)], axis=1)

    def key_rows(ref2d, j):
        return ref2d[pl.ds(pl.multiple_of(j * tk, tk), tk), :]

    qcT = heads_on_lanes(qc_ref[...]).astype(BF16)
    qrT = heads_on_lanes(qr_ref[...]).astype(BF16)
    delta = (lax.broadcasted_iota(jnp.int32, (tk, W), 0)
             - (lax.broadcasted_iota(jnp.int32, (tk, W), 1) & (tq - 1)))

    kw2 = kw_ref.at[0, 0]
    s_diag = _dot(key_rows(kw2, i), qrT)
    s_far = _dot(key_rows(kw2, jnp.maximum(i - (nwin - 1), 0)), qrT)
    sw_sc[0] = jnp.where(delta <= 0, s_diag, s_far)
    m_w = jnp.max(sw_sc[0], axis=0, keepdims=True)
    for d in range(1, nwin - 1):
        sw_sc[d] = _dot(key_rows(kw2, jnp.maximum(i - d, 0)), qrT)
        mt = jnp.max(sw_sc[d], axis=0, keepdims=True)
        m_w = jnp.maximum(m_w, jnp.where(i >= d, mt, NEG))
    pt = jnp.exp2(sw_sc[0] - m_w)
    far_thr = jnp.where(i >= nwin - 1, 0, 1 << 20)
    acc_w = _dot(vwT_ref[0, 0, i], jnp.where(delta <= 0, pt, 0.0).astype(BF16))
    acc_w = acc_w + _dot(vwT_ref[0, 0, jnp.maximum(i - (nwin - 1), 0)],
                         jnp.where(delta > far_thr, pt, 0.0).astype(BF16))
    for d in range(1, nwin - 1):
        pt = jnp.exp2(sw_sc[d] - (m_w + jnp.where(i >= d, 0.0, -NEG)))
        acc_w = acc_w + _dot(vwT_ref[0, 0, jnp.maximum(i - d, 0)], pt.astype(BF16))
    owin = acc_w[0:NSA_DK, :] * (1.0 / acc_w[V_ROWS - 1:V_ROWS, :])

    sc = _dot(kcmp_ref[0, 0], qcT)
    cidx = lax.broadcasted_iota(jnp.int32, (ncmp, W), 0)
    qpos = i * tq + (lax.broadcasted_iota(jnp.int32, (ncmp, W), 1) & (tq - 1))
    sc = jnp.where(cidx * CMP_STRIDE + (CMP_BLOCK - 1) <= qpos, sc, NEG)
    p = jnp.exp2(sc - jnp.max(sc, axis=0, keepdims=True))
    p = p * (1.0 / jnp.sum(p, axis=0, keepdims=True))
    p = p * jnp.where(qpos[0:1, :] >= CMP_BLOCK - 1, 1.0, 0.0)
    ocmp = _dot(vcmpT_ref[0, 0], p.astype(BF16))

    psum = p[:, 0:tq]
    for h in range(1, H):
        psum = psum + p[:, h * tq:(h + 1) * tq]
    nn = lax.broadcasted_iota(jnp.int32, (nsel, ncmp), 0) * SEL_BLOCK
    cc = lax.broadcasted_iota(jnp.int32, (nsel, ncmp), 1) * CMP_STRIDE
    ov = jnp.maximum(jnp.minimum(cc + CMP_BLOCK, nn + SEL_BLOCK) - jnp.maximum(cc, nn), 0)
    ov = (ov.astype(F32) * (1.0 / CMP_STRIDE)).astype(BF16)
    p_hi = psum.astype(BF16)
    r1 = psum - p_hi.astype(F32)
    p_mid = r1.astype(BF16)
    p_lo = (r1 - p_mid.astype(F32)).astype(BF16)
    imp = _dot(ov, p_hi) + _dot(ov, p_mid) + _dot(ov, p_lo)

    nidx = lax.broadcasted_iota(jnp.int32, (nsel, tq), 0)
    qblk = (i * tq + lax.broadcasted_iota(jnp.int32, (nsel, tq), 1)) // SEL_BLOCK
    dist = qblk - nidx
    forced = (nidx == 0) | ((dist >= 0) & (dist < SEL_LOCAL))
    imp = jnp.where(forced, SEL_FORCE, jnp.where(dist >= 0, imp, -1.0))
    rank = jnp.zeros((nsel, tq), jnp.int32)
    for m in range(nsel):
        rm = imp[m:m + 1, :]
        before = (rm > imp) | ((rm == imp) & (nidx > m))
        rank = rank + before.astype(jnp.int32)
    sel = (rank < SEL_TOPK) & (dist >= 0)
    bias = jnp.where(sel, 0.0, NEG).astype(BF16)
    bias = jnp.concatenate([bias] * H, axis=1)
    qaug = jnp.concatenate([qrT, bias, jnp.zeros((LANES - NSA_DK - nsel, W), BF16)], axis=0)
    qaug_sc[...] = qaug

    s = jnp.where(delta <= 0, _dot(key_rows(ksa_sc, i), qaug), NEG)
    m0 = jnp.max(s, axis=0, keepdims=True)
    m_sc[...] = m0
    alpha_sc[...] = jnp.ones((1, W), F32)
    acc_sc[...] = jnp.zeros((V_ROWS, W), F32)
    p_b[...] = jnp.exp2(s - m0).astype(BF16)
    s_a[...] = _dot(key_rows(ksa_sc, 0), qaug)

    def value_stage(jm, p_ref):
        acc_sc[...] = alpha_sc[...] * acc_sc[...] + _dot(vsT_ref[0, 0, jm], p_ref[...])

    def stages(j, s_cur, p_cur, s_nxt, p_prv):
        value_stage(jnp.where(j == 0, i, j - 1), p_prv)
        m_old = m_sc[...]
        m_new = jnp.maximum(m_old, jnp.max(s_cur[...], axis=0, keepdims=True))
        m_sc[...] = m_new
        alpha_sc[...] = jnp.exp2(m_old - m_new)
        p_cur[...] = jnp.exp2(s_cur[...] - m_new).astype(BF16)
        s_nxt[...] = _dot(key_rows(ksa_sc, jnp.minimum(j + 1, i - 1)), qaug_sc[...])

    def sel_body(j, carry):
        @pl.when((j & 1) == 0)
        def _():
            stages(j, s_a, p_a, s_b, p_b)

        @pl.when((j & 1) == 1)
        def _():
            stages(j, s_b, p_b, s_a, p_a)

        return carry

    lax.fori_loop(0, i, sel_body, 0)

    @pl.when((i & 1) == 0)
    def _():
        value_stage(jnp.where(i == 0, i, i - 1), p_b)

    @pl.when((i & 1) == 1)
    def _():
        value_stage(i - 1, p_a)

    acc = acc_sc[...]
    osel = acc[0:NSA_DK, :] * (1.0 / acc[V_ROWS - 1:V_ROWS, :])

    gate = gate_ref[0]
    outs = []
    for h in range(H):
        ls = slice(h * tq, (h + 1) * tq)
        outs.append(gate[3 * h:3 * h + 1, :] * ocmp[:, ls] + gate[3 * h + 1:3 * h + 2, :] * osel[:, ls]
                    + gate[3 * h + 2:3 * h + 3, :] * owin[:, ls])
    o_ref[...] = jnp.concatenate(outs, axis=0).T.astype(BF16)


def _nsa_attn(qc, qr, kcmp, vcmpT, ks, vsT, kw, vwT, gate, *, batch, seq):
    T, hd = qc.shape
    G = NSA_GROUPS
    tq = ATT_TILE
    nq = seq // tq
    gw = NSA_HPG * NSA_DK
    W = NSA_HPG * tq
    ncmp = kcmp.shape[2]
    nwin = WINDOW // tq + 1
    qspec = pl.BlockSpec((tq, gw), lambda b, g, i: (b * nq + i, g))
    rows4 = pl.BlockSpec((1, 1, seq, NSA_DK), lambda b, g, i: (b, g, 0, 0))
    cols5 = pl.BlockSpec((1, 1, seq // tq, V_ROWS, tq), lambda b, g, i: (b, g, 0, 0, 0))
    return pl.pallas_call(
        functools.partial(_nsa_attn_kernel, tq=tq, seq=seq),
        grid=(batch, G, nq),
        in_specs=[qspec, qspec,
                  pl.BlockSpec((1, 1, ncmp, NSA_DK), lambda b, g, i: (b, g, 0, 0)),
                  pl.BlockSpec((1, 1, NSA_DK, ncmp), lambda b, g, i: (b, g, 0, 0)),
                  rows4, cols5, rows4, cols5,
                  pl.BlockSpec((1, GATE_PAD, tq), lambda b, g, i: (g, 0, b * nq + i))],
        out_specs=qspec,
        out_shape=jax.ShapeDtypeStruct((T, hd), BF16),
        scratch_shapes=[pltpu.VMEM((seq, LANES), BF16), pltpu.VMEM((LANES, W), BF16),
                        pltpu.VMEM((tq, W), F32), pltpu.VMEM((tq, W), F32),
                        pltpu.VMEM((tq, W), BF16), pltpu.VMEM((tq, W), BF16), pltpu.VMEM((nwin - 1, tq, W), F32),
                        pltpu.VMEM((1, W), F32), pltpu.VMEM((1, W), F32), pltpu.VMEM((V_ROWS, W), F32)],
        compiler_params=_params(("arbitrary", "arbitrary", "arbitrary")),
        name="nsa_attn",
    )(qc, qr, kcmp, vcmpT, ks, vsT, kw, vwT, gate)


def _nsa_out_kernel(o_ref, z_ref, wout_ref, x_ref, gf_ref, y_ref, *, final):
    z = z_ref[...].astype(F32)
    u = o_ref[...].astype(F32) * (z * _sigmoid(z))
    y = x_ref[...] + _dot(u.astype(BF16), wout_ref[...])
    y_ref[...] = _rmsnorm(y, gf_ref[...]) if final else y


def _nsa_out(o, z, wout, x2, gf, *, final):
    T, hd = o.shape
    D = x2.shape[1]
    tm = NSA_ROWS
    row = lambda w: pl.BlockSpec((tm, w), lambda t: (t, 0))
    return pl.pallas_call(
        functools.partial(_nsa_out_kernel, final=final),
        grid=(T // tm,),
        in_specs=[row(hd), row(hd), _const_spec((hd, D)), row(D), _const_spec((1, D))],
        out_specs=row(D),
        out_shape=jax.ShapeDtypeStruct((T, D), F32),
        input_output_aliases={3: 0},
        compiler_params=_params(("arbitrary",)),
        name="nsa_out",
    )(o, z, wout, x2, gf)


def _final_norm_kernel(x_ref, g_ref, y_ref):
    y_ref[...] = _rmsnorm(x_ref[...], g_ref[...])


def _final_norm(x2, gf):
    T, D = x2.shape
    tm = NSA_ROWS
    row = pl.BlockSpec((tm, D), lambda t: (t, 0))
    return pl.pallas_call(
        _final_norm_kernel, grid=(T // tm,), in_specs=[row, _const_spec((1, D))], out_specs=row,
        out_shape=jax.ShapeDtypeStruct((T, D), F32), compiler_params=_params(("arbitrary",)),
        name="final_norm",
    )(x2, gf)


def _block_diag_tiles(w):
    nblk = w.shape[0]
    per = MXU_TILE // ML_QKV_BLK
    w4 = w.reshape(nblk // per, per, ML_QKV_BLK, ML_QKV_BLK)
    eye = jnp.eye(per, dtype=w.dtype)
    return jnp.einsum('cnij,nm->cnjmi', w4, eye).reshape(nblk // per, MXU_TILE, MXU_TILE).astype(BF16)


def _fold_headwise(w, w_if_part):
    nblk = w.shape[0]
    folded = jnp.einsum('nij,nio->njo', w, w_if_part.reshape(nblk, ML_QKV_BLK, -1), precision='highest')
    return folded.reshape(nblk * ML_QKV_BLK, -1)


def _mlstm_layer(x2, norm, w_in, conv_w, conv_b, w_q, w_k, w_v, w_if, b_if, ln_w, skip, w_out,
                 *, batch, seq, in_place):
    inner = w_in.shape[1] // 2
    pad = lambda m: jnp.zeros((m.shape[0], LANES), F32).at[:, :2 * ML_HEADS].set(m)
    gx = pad(_fold_headwise(w_q, w_if[:inner]) + _fold_headwise(w_k, w_if[inner:2 * inner])).astype(BF16)
    gi = pad(_fold_headwise(w_v, w_if[2 * inner:])).astype(BF16)
    q, kT, v, xc, z, gcol, grow = _ml_front(
        x2, norm[None, :], w_in.astype(BF16), conv_w, conv_b[None, :],
        _block_diag_tiles(w_q), _block_diag_tiles(w_k), _block_diag_tiles(w_v), gx, gi, pad(b_if[None, :]), seq=seq)
    hn = _ml_core(q, kT, v, gcol, grow, ln_w[None, :], batch=batch, seq=seq)
    return _ml_out(hn, xc, z, skip[None, :], w_out.astype(BF16), x2, in_place=in_place)


def _nsa_layer(x2, norm, w_in, b_gate, cmp_pe, cmp_w1, cmp_w2, w_out, rope_tabs, final_g, *, batch, seq, final):
    G, DK = NSA_GROUPS, NSA_DK
    hd = NSA_HEADS * DK
    base = 2 * hd + 6 * G * DK
    D = w_in.shape[0]
    src = jnp.arange(3 * NSA_HEADS)
    dst = (src // (3 * NSA_HPG)) * GATE_PAD + src % (3 * NSA_HPG)
    wg = jnp.zeros((D, LANES), F32).at[:, dst].set(w_in[:, base:])
    bg = jnp.zeros((1, LANES), F32).at[0, dst].set(b_gate)
    w = jnp.concatenate([w_in[:, :base], wg], axis=1).astype(BF16)
    qc, qr, kc, vc, ks, vsT, kw, vwT, z, gate = _nsa_front(x2, norm[None, :], w, bg, *rope_tabs,
                                                           batch=batch, seq=seq)
    half = CMP_STRIDE * DK
    w1ab = jnp.concatenate([cmp_w1[:, :half, :], cmp_w1[:, half:, :]], axis=2).astype(BF16)
    pe_flat = cmp_pe.reshape(2, 2, 1, half)
    pe2 = jnp.broadcast_to(pe_flat, (2, 2, 8, half)).reshape(2, 16, half).astype(BF16)
    w2p = jnp.zeros((2, cmp_w2.shape[1], LANES), F32).at[:, :, :DK].set(cmp_w2).astype(BF16)
    kcmp, vcmpT = _nsa_cmp(kc, vc, w1ab, pe2, w2p)
    o = _nsa_attn(qc, qr, kcmp, vcmpT, ks, vsT, kw, vwT, gate, batch=batch, seq=seq)
    return _nsa_out(o, z, w_out.astype(BF16), x2, final_g[None, :], final=final)


def _rope_tables(seq):
    half = ROPE_DIM // 2
    pos = jnp.arange(seq, dtype=F32)
    inv_freq = ROPE_THETA ** (-jnp.arange(0, ROPE_DIM, 2, dtype=F32) / ROPE_DIM)
    ang = pos[:, None] * inv_freq[None, :]
    cos, sin = jnp.cos(ang), jnp.sin(ang)
    lane = jnp.arange(LANES) % NSA_DK
    idx = lane % half
    ctab = jnp.where(lane[None, :] < ROPE_DIM, cos[:, idx], 1.0)
    s1 = jnp.where((lane[None, :] >= half) & (lane[None, :] < ROPE_DIM), sin[:, idx], 0.0)
    s2 = jnp.where(lane[None, :] < half, -sin[:, idx], 0.0)
    return ctab, s1, s2


def kernel(x, ml_norm, ml_w_in, ml_conv_w, ml_conv_b, ml_w_q, ml_w_k, ml_w_v, ml_w_if, ml_b_if, ml_ln_w, ml_skip,
           ml_w_out, nsa_norm, nsa_w_in, nsa_b_gate, nsa_cmp_pe, nsa_cmp_w1, nsa_cmp_w2, nsa_w_out, final_norm):
    batch, seq, d_model = x.shape
    depth = ml_norm.shape[0] + nsa_norm.shape[0]
    assert seq % ML_CHUNK == 0 and seq % NSA_ROWS == 0 and seq % ML_ROWS == 0 and WINDOW % ATT_TILE == 0
    x2 = x.reshape(batch * seq, d_model)
    rope_tabs = _rope_tables(seq)
    for i in range(depth):
        j = i // 2
        if i % 2 == 0:
            x2 = _mlstm_layer(x2, ml_norm[j], ml_w_in[j], ml_conv_w[j], ml_conv_b[j], ml_w_q[j], ml_w_k[j],
                              ml_w_v[j], ml_w_if[j], ml_b_if[j], ml_ln_w[j], ml_skip[j], ml_w_out[j],
                              batch=batch, seq=seq, in_place=(i > 0))
        else:
            x2 = _nsa_layer(x2, nsa_norm[j], nsa_w_in[j], nsa_b_gate[j], nsa_cmp_pe[j], nsa_cmp_w1[j],
                            nsa_cmp_w2[j], nsa_w_out[j], rope_tabs, final_norm, batch=batch, seq=seq,
                            final=(i == depth - 1))
    if depth % 2 == 1:
        x2 = _final_norm(x2, final_norm[None, :])
    return x2.reshape(batch, seq, d_model)
```

```python
import functools

import jax
import jax.numpy as jnp
from jax import lax
from jax.experimental import pallas as pl
from jax.experimental.pallas import tpu as pltpu

F32 = jnp.float32
BF16 = jnp.bfloat16

RMS_EPS = 1e-6
LN_EPS = 1e-6
NEG = -1e30

ML_HEADS = 4
ML_CONV = 4
ML_QKV_BLK = 4
ML_CHUNK = 256
ML_ROWS = 512

NSA_HEADS = 16
NSA_GROUPS = 4
NSA_HPG = NSA_HEADS // NSA_GROUPS
NSA_DK = 64
ROPE_DIM = NSA_DK // 4
ROPE_THETA = 500000.0
CMP_BLOCK = 32
CMP_STRIDE = 16
SEL_BLOCK = 64
SEL_TOPK = 16
SEL_LOCAL = 2
SEL_FORCE = 1e9
WINDOW = 512
NSA_ROWS = 512
ATT_TILE = 256
GATE_PAD = 16
V_ROWS = NSA_DK + 16
LOG2E = 1.4426950408889634

LANES = 128
MXU_TILE = 256
VMEM_LIMIT = 56 * 1024 * 1024


def _dot(a, b):
    return jnp.dot(a, b, preferred_element_type=F32)


def _sigmoid(v):
    return 1.0 / (1.0 + jnp.exp(-v))


def _const_spec(shape):
    zeros = (0,) * len(shape)
    return pl.BlockSpec(shape, lambda *_: zeros, pipeline_mode=pl.Buffered(1))


def _params(sem, flags=None):
    return pltpu.CompilerParams(dimension_semantics=sem, vmem_limit_bytes=VMEM_LIMIT, flags=flags)


def _rmsnorm(x, g):
    ms = jnp.mean(x * x, axis=-1, keepdims=True)
    return (x * lax.rsqrt(ms + RMS_EPS)) * g


def _ml_front_kernel(x_ref, g_ref, win_ref, convw_ref, convb_ref, wq_ref, wk_ref, wv_ref, gx_ref, gi_ref, bif_ref,
                     q_ref, kT_ref, v_ref, xc_ref, z_ref, gcol_ref, grow_ref, xbuf, h_sc,
                     *, tm, tiles_per_seq, inner, q_scale):
    t = pl.program_id(0)
    h_sc[...] = _rmsnorm(x_ref[...], g_ref[...]).astype(BF16)

    @pl.when(t % tiles_per_seq == 0)
    def _():
        xbuf[:, 0:8, :] = jnp.zeros((inner // MXU_TILE, 8, MXU_TILE), F32)

    nchunk = inner // MXU_TILE

    def in_proj(c):
        sl = slice(c * MXU_TILE, (c + 1) * MXU_TILE)
        xbuf[c, 8:8 + tm, :] = _dot(h_sc[...], win_ref[:, sl])
        z_ref[:, sl] = _dot(h_sc[...], win_ref[:, inner + c * MXU_TILE:inner + (c + 1) * MXU_TILE]).astype(BF16)

    in_proj(0)
    gates = jnp.broadcast_to(bif_ref[...], (tm, LANES))
    for c in range(nchunk):
        sl = slice(c * MXU_TILE, (c + 1) * MXU_TILE)
        if c + 1 < nchunk:
            in_proj(c + 1)
        conv = convb_ref[:, sl]
        for tap in range(ML_CONV):
            lo = 8 - (ML_CONV - 1) + tap
            conv = conv + xbuf[c, lo:lo + tm, :] * convw_ref[tap:tap + 1, sl]
        xin_b = xbuf[c, 8:8 + tm, :].astype(BF16)
        xbuf[c, 0:8, :] = xbuf[c, tm:tm + 8, :]
        xc_b = (conv * _sigmoid(conv)).astype(BF16)
        xc_ref[:, sl] = xc_b
        q_ref[:, sl] = (_dot(xc_b, wq_ref[c]) * q_scale).astype(BF16)
        kT_ref[sl, :] = _dot(xc_b, wk_ref[c]).T.astype(BF16)
        v_ref[:, sl] = _dot(xin_b, wv_ref[c]).astype(BF16)
        gates = gates + _dot(xc_b, gx_ref[sl, :]) + _dot(xin_b, gi_ref[sl, :])

    logsig = -(jnp.maximum(-gates, 0.0) + jnp.log1p(jnp.exp(-jnp.abs(gates))))
    lane = lax.broadcasted_iota(jnp.int32, (tm, LANES), 1)
    gsel = jnp.where(lane < ML_HEADS, gates, logsig)
    gcol_ref[...] = gsel[:, 0:2 * ML_HEADS]
    grow_ref[...] = gsel.T[0:2 * ML_HEADS, :]


def _ml_front(x2, g, win, convw, convb, wq, wk, wv, gx, gi, bif, *, seq):
    T, D = x2.shape
    inner = win.shape[1] // 2
    tm = ML_ROWS
    nq = inner // MXU_TILE
    kern = functools.partial(_ml_front_kernel, tm=tm, tiles_per_seq=seq // tm, inner=inner,
                             q_scale=float((inner // ML_HEADS) ** -0.5))
    row = lambda w: pl.BlockSpec((tm, w), lambda t: (t, 0))
    bd = _const_spec((nq, MXU_TILE, MXU_TILE))
    return pl.pallas_call(
        kern,
        grid=(T // tm,),
        in_specs=[row(D), _const_spec((1, D)), _const_spec((D, 2 * inner)),
                  _const_spec((ML_CONV, inner)), _const_spec((1, inner)), bd, bd, bd,
                  _const_spec((inner, LANES)), _const_spec((inner, LANES)), _const_spec((1, LANES))],
        out_specs=[row(inner), pl.BlockSpec((inner, tm), lambda t: (0, t)), row(inner),
                   row(inner), row(inner), row(2 * ML_HEADS),
                   pl.BlockSpec((2 * ML_HEADS, tm), lambda t: (0, t))],
        out_shape=[jax.ShapeDtypeStruct((T, inner), BF16),
                   jax.ShapeDtypeStruct((inner, T), BF16), jax.ShapeDtypeStruct((T, inner), BF16),
                   jax.ShapeDtypeStruct((T, inner), BF16), jax.ShapeDtypeStruct((T, inner), BF16),
                   jax.ShapeDtypeStruct((T, 2 * ML_HEADS), F32), jax.ShapeDtypeStruct((2 * ML_HEADS, T), F32)],
        scratch_shapes=[pltpu.VMEM((nq, tm + 8, MXU_TILE), F32), pltpu.VMEM((tm, D), BF16)],
        compiler_params=_params(("arbitrary",)),
        name="ml_front",
    )(x2, g, win, convw, convb, wq, wk, wv, gx, gi, bif)


def _ml_core_kernel(q_ref, kT_ref, v_ref, gcol_ref, grow_ref, lnw_ref, o_ref, s_sc, n_sc, m_sc,
                    qk_sc, qs_sc, qn_sc, *, L, dh):
    c = pl.program_id(1)
    ones = jnp.ones((L, LANES), BF16)

    @pl.when(c == 0)
    def _():
        s_sc[...] = jnp.zeros_like(s_sc)
        n_sc[...] = jnp.zeros_like(n_sc)
        m_sc[...] = jnp.zeros_like(m_sc)

    gcol = gcol_ref[...]
    grow = grow_ref[...]
    row = lax.broadcasted_iota(jnp.int32, (L, L), 0)
    col = lax.broadcasted_iota(jnp.int32, (L, L), 1)
    causal = col <= row
    def query_matmuls(h):
        sl = slice(h * dh, (h + 1) * dh)
        q = q_ref[:, sl]
        qk_sc[h % 2] = _dot(q, kT_ref[sl, :])
        qs_sc[h % 2] = _dot(q, s_sc[h].astype(BF16))
        qn_sc[h % 2] = _dot(q, n_sc[h].astype(BF16))

    query_matmuls(0)
    for h in range(ML_HEADS):
        if h + 1 < ML_HEADS:
            query_matmuls(h + 1)
        sl = slice(h * dh, (h + 1) * dh)
        v = v_ref[:, sl]
        kT = kT_ref[sl, :]
        fc = gcol[:, ML_HEADS + h:ML_HEADS + h + 1]
        ir = grow[h:h + 1, :]
        fr = grow[ML_HEADS + h:ML_HEADS + h + 1, :]
        m_prev = m_sc[h][:, 0:1]
        b_col = jnp.sum(jnp.where(causal, fr, 0.0), axis=1, keepdims=True)
        b_row = jnp.sum(jnp.where(row <= col, fc, 0.0), axis=0, keepdims=True)
        dmat = jnp.where(causal, b_col - b_row + ir, -jnp.inf)
        inter = b_col + m_prev
        m_row = jnp.maximum(inter, jnp.max(dmat, axis=1, keepdims=True))
        w_intra = jnp.exp(dmat - m_row)
        w_inter = jnp.exp(inter - m_row)
        s = qk_sc[h % 2] * w_intra
        num = _dot(s.astype(BF16), v) + w_inter * qs_sc[h % 2]
        den = jnp.sum(s, axis=1, keepdims=True) + w_inter * qn_sc[h % 2][:, 0:1]
        hh = num * (1.0 / jnp.maximum(jnp.abs(den), jnp.exp(-m_row)))
        mu = jnp.mean(hh, axis=1, keepdims=True)
        cen = hh - mu
        var = jnp.mean(cen * cen, axis=1, keepdims=True)
        o_ref[:, sl] = ((cen * lax.rsqrt(var + LN_EPS)) * lnw_ref[:, sl]).astype(BF16)

        b_last = b_col[L - 1:L, :]
        a_row = b_last - b_row + ir
        m_new = jnp.maximum(b_last + m_prev, jnp.max(a_row, axis=1, keepdims=True))
        decay = jnp.exp(b_last + m_prev - m_new)
        kw = kT * jnp.exp(a_row - m_new).astype(BF16)
        s_sc[h] = decay * s_sc[h] + _dot(kw, v)
        n_sc[h] = decay * n_sc[h] + _dot(kw, ones)
        m_sc[h] = jnp.broadcast_to(m_new, (1, LANES))


def _ml_core(q, kT, v, gcol, grow, lnw, *, batch, seq):
    T, inner = q.shape
    L = ML_CHUNK
    nc = seq // L
    dh = inner // ML_HEADS
    kern = functools.partial(_ml_core_kernel, L=L, dh=dh)
    row = lambda w: pl.BlockSpec((L, w), lambda b, c: (b * nc + c, 0))
    colb = lambda r: pl.BlockSpec((r, L), lambda b, c: (0, b * nc + c))
    return pl.pallas_call(
        kern,
        grid=(batch, nc),
        in_specs=[row(inner), colb(inner), row(inner), row(2 * ML_HEADS), colb(2 * ML_HEADS),
                  _const_spec((1, inner))],
        out_specs=row(inner),
        out_shape=jax.ShapeDtypeStruct((T, inner), BF16),
        scratch_shapes=[pltpu.VMEM((ML_HEADS, dh, dh), F32), pltpu.VMEM((ML_HEADS, dh, LANES), F32),
                        pltpu.VMEM((ML_HEADS, 1, LANES), F32), pltpu.VMEM((2, L, L), F32),
                        pltpu.VMEM((2, L, dh), F32), pltpu.VMEM((2, L, LANES), F32)],
        compiler_params=_params(("arbitrary", "arbitrary")),
        name="ml_core",
    )(q, kT, v, gcol, grow, lnw)


def _ml_out_kernel(hn_ref, xc_ref, z_ref, skip_ref, wout_ref, x_ref, o_ref, u_sc):
    nk = hn_ref.shape[1] // MXU_TILE

    def gated(k):
        sl = slice(k * MXU_TILE, (k + 1) * MXU_TILE)
        z = z_ref[:, sl].astype(F32)
        u = (hn_ref[:, sl].astype(F32) + skip_ref[:, sl] * xc_ref[:, sl].astype(F32)) * (z * _sigmoid(z))
        u_sc[k % 2] = u.astype(BF16)

    gated(0)
    for k in range(nk):
        if k + 1 < nk:
            gated(k + 1)
        part = _dot(u_sc[k % 2], wout_ref[k * MXU_TILE:(k + 1) * MXU_TILE, :])
        o_ref[...] = (x_ref[...] if k == 0 else o_ref[...]) + part


def _ml_out(hn, xc, z, skip, wout, x2, *, in_place):
    T, inner = hn.shape
    D = x2.shape[1]
    tm = ML_ROWS
    row = lambda w: pl.BlockSpec((tm, w), lambda t: (t, 0))
    return pl.pallas_call(
        _ml_out_kernel,
        grid=(T // tm,),
        in_specs=[row(inner), row(inner), row(inner), _const_spec((1, inner)), _const_spec((inner, D)), row(D)],
        out_specs=row(D),
        out_shape=jax.ShapeDtypeStruct((T, D), F32),
        input_output_aliases={5: 0} if in_place else {},
        scratch_shapes=[pltpu.VMEM((2, tm, MXU_TILE), BF16)],
        compiler_params=_params(("arbitrary",)),
        name="ml_out",
    )(hn, xc, z, skip, wout, x2)


def _rope(a, cos, s1, s2):
    return a * cos + pltpu.roll(a, ROPE_DIM // 2, 1) * s1 + pltpu.roll(a, LANES - ROPE_DIM // 2, 1) * s2


def _nsa_front_kernel(x_ref, g_ref, w_ref, bg_ref, cos_ref, s1_ref, s2_ref,
                      qc_ref, qr_ref, kc_ref, vc_ref, ks_ref, kw_ref, vT_ref, z_ref, gate_ref,
                      h_sc, sec_sc, *, tm, hd, gd):
    h_sc[...] = _rmsnorm(x_ref[...], g_ref[...]).astype(BF16)
    cos, s1, s2 = cos_ref[...], s1_ref[...], s2_ref[...]
    scale = NSA_DK ** -0.5 * LOG2E
    G = NSA_GROUPS

    def slabs_of(sec, rope):
        slabs = [sec[:, c * LANES:(c + 1) * LANES] for c in range(sec.shape[1] // LANES)]
        return [_rope(a, cos, s1, s2) for a in slabs] if rope else slabs

    def store_q(c):
        def run(sec):
            for half, a in enumerate(slabs_of(sec, False)):
                sl = slice(c * MXU_TILE + half * LANES, c * MXU_TILE + (half + 1) * LANES)
                qc_ref[:, sl] = (a * scale).astype(BF16)
                qr_ref[:, sl] = (_rope(a, cos, s1, s2) * scale).astype(BF16)
        return run

    def store_rows(ref, rope):
        def run(sec):
            slabs = slabs_of(sec, rope)
            for g in range(G):
                lo = (g * NSA_DK) % LANES
                ref[0, g] = slabs[(g * NSA_DK) // LANES][:, lo:lo + NSA_DK].astype(ref.dtype)
        return run

    def store_cols(br):
        def run(sec):
            slabs = slabs_of(sec, False)
            ones = jnp.ones((V_ROWS - NSA_DK, ATT_TILE), BF16)
            for g in range(G):
                aT = slabs[(g * NSA_DK) // LANES].T
                lo = (g * NSA_DK) % LANES
                for j in range(tm // ATT_TILE):
                    vT_ref[0, g, br, j, 0:NSA_DK, :] = (
                        aT[lo:lo + NSA_DK, j * ATT_TILE:(j + 1) * ATT_TILE].astype(BF16))
                    vT_ref[0, g, br, j, NSA_DK:V_ROWS, :] = ones
        return run

    def store_z(c):
        def run(sec):
            z_ref[:, c * MXU_TILE:(c + 1) * MXU_TILE] = sec.astype(BF16)
        return run

    def store_gate(sec):
        gT = _sigmoid(sec + bg_ref[...]).T
        for g in range(G):
            gate_ref[g] = gT[g * GATE_PAD:(g + 1) * GATE_PAD, :]

    zoff = hd + 6 * gd
    sections = ([(c * MXU_TILE, MXU_TILE, store_q(c)) for c in range(hd // MXU_TILE)]
                + [(hd, gd, store_rows(kc_ref, False)), (hd + gd, gd, store_rows(vc_ref, False)),
                   (hd + 2 * gd, gd, store_rows(ks_ref, True)), (hd + 3 * gd, gd, store_cols(0)),
                   (hd + 4 * gd, gd, store_rows(kw_ref, True)), (hd + 5 * gd, gd, store_cols(1))]
                + [(zoff + c * MXU_TILE, MXU_TILE, store_z(c)) for c in range(hd // MXU_TILE)]
                + [(zoff + hd, LANES, store_gate)])

    def project(k):
        off, width, _ = sections[k]
        sec_sc[k % 2, :, 0:width] = _dot(h_sc[...], w_ref[:, off:off + width])

    project(0)
    for k, (_, width, consume) in enumerate(sections):
        if k + 1 < len(sections):
            project(k + 1)
        consume(sec_sc[k % 2, :, 0:width])


def _nsa_front(x2, g, w, bg, cos, s1, s2, *, batch, seq):
    T, D = x2.shape
    tm = NSA_ROWS
    tps = seq // tm
    G = NSA_GROUPS
    hd = NSA_HEADS * NSA_DK
    gd = G * NSA_DK
    ntk = seq // ATT_TILE
    kern = functools.partial(_nsa_front_kernel, tm=tm, hd=hd, gd=gd)
    row = lambda w_: pl.BlockSpec((tm, w_), lambda t: (t, 0))
    tab = pl.BlockSpec((tm, LANES), lambda t: (t % tps, 0))
    rows4 = pl.BlockSpec((1, G, tm, NSA_DK), lambda t: (t // tps, 0, t % tps, 0))
    cols6 = pl.BlockSpec((1, G, 2, tm // ATT_TILE, V_ROWS, ATT_TILE), lambda t: (t // tps, 0, 0, t % tps, 0, 0))
    rows_shape = jax.ShapeDtypeStruct((batch, G, seq, NSA_DK), BF16)
    rows_f32 = jax.ShapeDtypeStruct((batch, G, seq, NSA_DK), F32)
    cols_shape = jax.ShapeDtypeStruct((batch, G, 2, ntk, V_ROWS, ATT_TILE), BF16)
    return pl.pallas_call(
        kern,
        grid=(T // tm,),
        in_specs=[row(D), _const_spec((1, D)), _const_spec(w.shape), _const_spec((1, LANES)), tab, tab, tab],
        out_specs=[row(hd), row(hd), rows4, rows4, rows4, rows4, cols6, row(hd),
                   pl.BlockSpec((G, GATE_PAD, tm), lambda t: (0, 0, t))],
        out_shape=[jax.ShapeDtypeStruct((T, hd), BF16), jax.ShapeDtypeStruct((T, hd), BF16),
                   rows_f32, rows_f32, rows_shape, rows_shape, cols_shape,
                   jax.ShapeDtypeStruct((T, hd), BF16), jax.ShapeDtypeStruct((G, GATE_PAD, T), F32)],
        scratch_shapes=[pltpu.VMEM((tm, D), BF16), pltpu.VMEM((2, tm, MXU_TILE), F32)],
        compiler_params=_params(("arbitrary",)),
        name="nsa_front",
    )(x2, g, w, bg, cos, s1, s2)


def _nsa_cmp_kernel(kc_ref, vc_ref, w1_ref, pe_ref, w2_ref, kcmp_ref, vcmpT_ref, *, hid, n):
    def mlp(x_ref, i):
        ab = jnp.zeros((n, 2 * hid), F32)
        for r in range(CMP_STRIDE):
            xr = x_ref[0, 0, pl.ds(r, n, stride=CMP_STRIDE), :]
            ab = ab + _dot(xr.astype(BF16), w1_ref[i, r * NSA_DK:(r + 1) * NSA_DK, :])
        pb = _dot(pe_ref[i], w1_ref[i])
        bias = pb[0:1, 0:hid] + pb[8:9, hid:2 * hid]
        h1 = ab[:, 0:hid] + pltpu.roll(ab[:, hid:2 * hid], n - 1, 0) + bias
        return _dot((h1 * _sigmoid(h1)).astype(BF16), w2_ref[i])

    kcmp_ref[0, 0] = mlp(kc_ref, 0)[:, 0:NSA_DK].astype(BF16)
    vcmpT_ref[0, 0] = mlp(vc_ref, 1).T[0:NSA_DK, :].astype(BF16)


def _nsa_cmp(kc, vc, w1ab, pe2, w2p):
    B, G, seq, width = kc.shape
    n = seq // CMP_STRIDE
    hid = w1ab.shape[2] // 2
    blk = pl.BlockSpec((1, 1, seq, width), lambda b, g: (b, g, 0, 0))
    return pl.pallas_call(
        functools.partial(_nsa_cmp_kernel, hid=hid, n=n),
        grid=(B, G),
        in_specs=[blk, blk, _const_spec(w1ab.shape), _const_spec(pe2.shape), _const_spec(w2p.shape)],
        out_specs=[pl.BlockSpec((1, 1, n, NSA_DK), lambda b, g: (b, g, 0, 0)),
                   pl.BlockSpec((1, 1, NSA_DK, n), lambda b, g: (b, g, 0, 0))],
        out_shape=[jax.ShapeDtypeStruct((B, G, n, NSA_DK), BF16), jax.ShapeDtypeStruct((B, G, NSA_DK, n), BF16)],
        compiler_params=_params(("arbitrary", "arbitrary")),
        name="nsa_cmp",
    )(kc, vc, w1ab, pe2, w2p)


def _nsa_attn_kernel(qc_ref, qr_ref, kcmp_ref, vcmpT_ref, ks_ref, kw_ref, vT_ref, gate_ref, o_ref,
                     ksa_sc, qaug_sc, s_a, s_b, p_a, p_b, sw_sc, m_sc, alpha_sc, acc_sc, *, tq, seq):
    i = pl.program_id(2)
    H = NSA_HPG
    W = H * tq
    tk = tq
    nsel = seq // SEL_BLOCK
    ncmp = kcmp_ref.shape[2]
    nwin = WINDOW // tk + 1

    @pl.when(i == 0)
    def _():
        ksa_sc[:, 0:NSA_DK] = ks_ref[0, 0]
        kb = lax.broadcasted_iota(jnp.int32, (seq, LANES - NSA_DK), 0) // SEL_BLOCK
        nn = lax.broadcasted_iota(jnp.int32, (seq, LANES - NSA_DK), 1)
        ksa_sc[:, NSA_DK:LANES] = jnp.where(kb == nn, 1.0, 0.0).astype(BF16)

    def heads_on_lanes(q_tile):
        qT = q_tile.astype(F32).T
        return jnp.concatenate([qT[h * NSA_DK:(h + 1) * NSA_DK, :] for h in range(H)], axis=1)

    def key_rows(ref2d, j):
        return ref2d[pl.ds(pl.multiple_of(j * tk, tk), tk), :]

    qcT = heads_on_lanes(qc_ref[...]).astype(BF16)
    qrT = heads_on_lanes(qr_ref[...]).astype(BF16)
    delta = (lax.broadcasted_iota(jnp.int32, (tk, W), 0)
             - (lax.broadcasted_iota(jnp.int32, (tk, W), 1) & (tq - 1)))

    kw2 = kw_ref.at[0, 0]
    s_diag = _dot(key_rows(kw2, i), qrT)
    s_far = _dot(key_rows(kw2, jnp.maximum(i - (nwin - 1), 0)), qrT)
    sw_sc[0] = jnp.where(delta <= 0, s_diag, s_far)
    m_w = jnp.max(sw_sc[0], axis=0, keepdims=True)
    for d in range(1, nwin - 1):
        sw_sc[d] = _dot(key_rows(kw2, jnp.maximum(i - d, 0)), qrT)
        mt = jnp.max(sw_sc[d], axis=0, keepdims=True)
        m_w = jnp.maximum(m_w, jnp.where(i >= d, mt, NEG))
    pt = jnp.exp2(sw_sc[0] - m_w)
    far_thr = jnp.where(i >= nwin - 1, 0, 1 << 20)
    acc_w = _dot(vT_ref[0, 0, 1, i], jnp.where(delta <= 0, pt, 0.0).astype(BF16))
    acc_w = acc_w + _dot(vT_ref[0, 0, 1, jnp.maximum(i - (nwin - 1), 0)],
                         jnp.where(delta > far_thr, pt, 0.0).astype(BF16))
    for d in range(1, nwin - 1):
        pt = jnp.exp2(sw_sc[d] - (m_w + jnp.where(i >= d, 0.0, -NEG)))
        acc_w = acc_w + _dot(vT_ref[0, 0, 1, jnp.maximum(i - d, 0)], pt.astype(BF16))
    owin = acc_w[0:NSA_DK, :] * (1.0 / acc_w[V_ROWS - 1:V_ROWS, :])

    sc = _dot(kcmp_ref[0, 0], qcT)
    cidx = lax.broadcasted_iota(jnp.int32, (ncmp, W), 0)
    qpos = i * tq + (lax.broadcasted_iota(jnp.int32, (ncmp, W), 1) & (tq - 1))
    sc = jnp.where(cidx * CMP_STRIDE + (CMP_BLOCK - 1) <= qpos, sc, NEG)
    p = jnp.exp2(sc - jnp.max(sc, axis=0, keepdims=True))
    p = p * (1.0 / jnp.sum(p, axis=0, keepdims=True))
    p = p * jnp.where(qpos[0:1, :] >= CMP_BLOCK - 1, 1.0, 0.0)
    ocmp = _dot(vcmpT_ref[0, 0], p.astype(BF16))

    psum = p[:, 0:tq]
    for h in range(1, H):
        psum = psum + p[:, h * tq:(h + 1) * tq]
    nn = lax.broadcasted_iota(jnp.int32, (nsel, ncmp), 0) * SEL_BLOCK
    cc = lax.broadcasted_iota(jnp.int32, (nsel, ncmp), 1) * CMP_STRIDE
    ov = jnp.maximum(jnp.minimum(cc + CMP_BLOCK, nn + SEL_BLOCK) - jnp.maximum(cc, nn), 0)
    ov = (ov.astype(F32) * (1.0 / CMP_STRIDE)).astype(BF16)
    p_hi = psum.astype(BF16)
    r1 = psum - p_hi.astype(F32)
    p_mid = r1.astype(BF16)
    p_lo = (r1 - p_mid.astype(F32)).astype(BF16)
    imp = _dot(ov, p_hi) + _dot(ov, p_mid) + _dot(ov, p_lo)

    nidx = lax.broadcasted_iota(jnp.int32, (nsel, tq), 0)
    qblk = (i * tq + lax.broadcasted_iota(jnp.int32, (nsel, tq), 1)) // SEL_BLOCK
    dist = qblk - nidx
    forced = (nidx == 0) | ((dist >= 0) & (dist < SEL_LOCAL))
    imp = jnp.where(forced, SEL_FORCE, jnp.where(dist >= 0, imp, -1.0))
    rank = jnp.zeros((nsel, tq), jnp.int32)
    for m in range(nsel):
        rm = imp[m:m + 1, :]
        before = (rm > imp) | ((rm == imp) & (nidx > m))
        rank = rank + before.astype(jnp.int32)
    sel = (rank < SEL_TOPK) & (dist >= 0)
    bias = jnp.where(sel, 0.0, NEG).astype(BF16)
    bias = jnp.concatenate([bias] * H, axis=1)
    qaug = jnp.concatenate([qrT, bias, jnp.zeros((LANES - NSA_DK - nsel, W), BF16)], axis=0)
    qaug_sc[...] = qaug

    s = jnp.where(delta <= 0, _dot(key_rows(ksa_sc, i), qaug), NEG)
    m0 = jnp.max(s, axis=0, keepdims=True)
    m_sc[...] = m0
    alpha_sc[...] = jnp.ones((1, W), F32)
    acc_sc[...] = jnp.zeros((V_ROWS, W), F32)
    p_b[...] = jnp.exp2(s - m0).astype(BF16)
    s_a[...] = _dot(key_rows(ksa_sc, 0), qaug)

    def value_stage(jm, p_ref):
        vT = vT_ref[0, 0, 0, jm]
        for h in range(H):
            ls = slice(h * tq, (h + 1) * tq)
            acc_sc[:, ls] = alpha_sc[:, ls] * acc_sc[:, ls] + _dot(vT, p_ref[:, ls])

    def stages(j, s_cur, p_cur, s_nxt, p_prv):
        vT = vT_ref[0, 0, 0, jnp.where(j == 0, i, j - 1)]
        k_nxt = key_rows(ksa_sc, jnp.minimum(j + 1, i - 1))
        for h in range(H):
            ls = slice(h * tq, (h + 1) * tq)
            acc_sc[:, ls] = alpha_sc[:, ls] * acc_sc[:, ls] + _dot(vT, p_prv[:, ls])
            m_old = m_sc[:, ls]
            m_new = jnp.maximum(m_old, jnp.max(s_cur[:, ls], axis=0, keepdims=True))
            m_sc[:, ls] = m_new
            alpha_sc[:, ls] = jnp.exp2(m_old - m_new)
            p_cur[:, ls] = jnp.exp2(s_cur[:, ls] - m_new).astype(BF16)
            s_nxt[:, ls] = _dot(k_nxt, qaug_sc[:, ls])

    def sel_body(j, carry):
        @pl.when((j & 1) == 0)
        def _():
            stages(j, s_a, p_a, s_b, p_b)

        @pl.when((j & 1) == 1)
        def _():
            stages(j, s_b, p_b, s_a, p_a)

        return carry

    lax.fori_loop(0, i, sel_body, 0)

    @pl.when((i & 1) == 0)
    def _():
        value_stage(jnp.where(i == 0, i, i - 1), p_b)

    @pl.when((i & 1) == 1)
    def _():
        value_stage(i - 1, p_a)

    acc = acc_sc[...]
    osel = acc[0:NSA_DK, :] * (1.0 / acc[V_ROWS - 1:V_ROWS, :])

    gate = gate_ref[0]
    outs = []
    for h in range(H):
        ls = slice(h * tq, (h + 1) * tq)
        outs.append(gate[3 * h:3 * h + 1, :] * ocmp[:, ls] + gate[3 * h + 1:3 * h + 2, :] * osel[:, ls]
                    + gate[3 * h + 2:3 * h + 3, :] * owin[:, ls])
    o_ref[...] = jnp.concatenate(outs, axis=0).T.astype(BF16)


def _nsa_attn(qc, qr, kcmp, vcmpT, ks, kw, vT, gate, *, batch, seq):
    T, hd = qc.shape
    G = NSA_GROUPS
    tq = ATT_TILE
    nq = seq // tq
    gw = NSA_HPG * NSA_DK
    W = NSA_HPG * tq
    ncmp = kcmp.shape[2]
    nwin = WINDOW // tq + 1
    qspec = pl.BlockSpec((tq, gw), lambda b, g, i: (b * nq + i, g))
    rows4 = pl.BlockSpec((1, 1, seq, NSA_DK), lambda b, g, i: (b, g, 0, 0))
    return pl.pallas_call(
        functools.partial(_nsa_attn_kernel, tq=tq, seq=seq),
        grid=(batch, G, nq),
        in_specs=[qspec, qspec,
                  pl.BlockSpec((1, 1, ncmp, NSA_DK), lambda b, g, i: (b, g, 0, 0)),
                  pl.BlockSpec((1, 1, NSA_DK, ncmp), lambda b, g, i: (b, g, 0, 0)),
                  rows4, rows4,
                  pl.BlockSpec((1, 1, 2, nq, V_ROWS, tq), lambda b, g, i: (b, g, 0, 0, 0, 0)),
                  pl.BlockSpec((1, GATE_PAD, tq), lambda b, g, i: (g, 0, b * nq + i))],
        out_specs=qspec,
        out_shape=jax.ShapeDtypeStruct((T, hd), BF16),
        scratch_shapes=[pltpu.VMEM((seq, LANES), BF16), pltpu.VMEM((LANES, W), BF16),
                        pltpu.VMEM((tq, W), F32), pltpu.VMEM((tq, W), F32),
                        pltpu.VMEM((tq, W), BF16), pltpu.VMEM((tq, W), BF16), pltpu.VMEM((nwin - 1, tq, W), F32),
                        pltpu.VMEM((1, W), F32), pltpu.VMEM((1, W), F32), pltpu.VMEM((V_ROWS, W), F32)],
        compiler_params=_params(("arbitrary", "arbitrary", "arbitrary")),
        name="nsa_attn",
    )(qc, qr, kcmp, vcmpT, ks, kw, vT, gate)


def _nsa_out_kernel(o_ref, z_ref, wout_ref, x_ref, gf_ref, y_ref, u_sc, *, final):
    nk = o_ref.shape[1] // MXU_TILE

    def gated(k):
        sl = slice(k * MXU_TILE, (k + 1) * MXU_TILE)
        z = z_ref[:, sl].astype(F32)
        u_sc[k % 2] = (o_ref[:, sl].astype(F32) * (z * _sigmoid(z))).astype(BF16)

    gated(0)
    for k in range(nk):
        if k + 1 < nk:
            gated(k + 1)
        part = _dot(u_sc[k % 2], wout_ref[k * MXU_TILE:(k + 1) * MXU_TILE, :])
        y = (x_ref[...] if k == 0 else y_ref[...]) + part
        y_ref[...] = _rmsnorm(y, gf_ref[...]) if (final and k == nk - 1) else y


def _nsa_out(o, z, wout, x2, gf, *, final):
    T, hd = o.shape
    D = x2.shape[1]
    tm = NSA_ROWS
    row = lambda w: pl.BlockSpec((tm, w), lambda t: (t, 0))
    return pl.pallas_call(
        functools.partial(_nsa_out_kernel, final=final),
        grid=(T // tm,),
        in_specs=[row(hd), row(hd), _const_spec((hd, D)), row(D), _const_spec((1, D))],
        out_specs=row(D),
        out_shape=jax.ShapeDtypeStruct((T, D), F32),
        input_output_aliases={3: 0},
        scratch_shapes=[pltpu.VMEM((2, tm, MXU_TILE), BF16)],
        compiler_params=_params(("arbitrary",)),
        name="nsa_out",
    )(o, z, wout, x2, gf)


def _final_norm_kernel(x_ref, g_ref, y_ref):
    y_ref[...] = _rmsnorm(x_ref[...], g_ref[...])


def _final_norm(x2, gf):
    T, D = x2.shape
    tm = NSA_ROWS
    row = pl.BlockSpec((tm, D), lambda t: (t, 0))
    return pl.pallas_call(
        _final_norm_kernel, grid=(T // tm,), in_specs=[row, _const_spec((1, D))], out_specs=row,
        out_shape=jax.ShapeDtypeStruct((T, D), F32), compiler_params=_params(("arbitrary",)),
        name="final_norm",
    )(x2, gf)


def _block_diag_tiles(w):
    nblk = w.shape[0]
    per = MXU_TILE // ML_QKV_BLK
    rows = jnp.swapaxes(w, 1, 2).reshape(nblk // per, MXU_TILE, ML_QKV_BLK)
    tiled = jnp.tile(rows, (1, 1, per))
    blk = jnp.arange(MXU_TILE) // ML_QKV_BLK
    return jnp.where(blk[:, None] == blk[None, :], tiled, 0.0).astype(BF16)


def _fold_headwise(w, w_if_part):
    nblk = w.shape[0]
    folded = jnp.einsum('nij,nio->njo', w, w_if_part.reshape(nblk, ML_QKV_BLK, -1), precision='highest')
    return folded.reshape(nblk * ML_QKV_BLK, -1)


def _mlstm_layer(x2, norm, w_in, conv_w, conv_b, w_q, w_k, w_v, w_if, b_if, ln_w, skip, w_out,
                 *, batch, seq, in_place):
    inner = w_in.shape[1] // 2
    pad = lambda m: jnp.zeros((m.shape[0], LANES), F32).at[:, :2 * ML_HEADS].set(m)
    gx = pad(_fold_headwise(w_q, w_if[:inner]) + _fold_headwise(w_k, w_if[inner:2 * inner])).astype(BF16)
    gi = pad(_fold_headwise(w_v, w_if[2 * inner:])).astype(BF16)
    q, kT, v, xc, z, gcol, grow = _ml_front(
        x2, norm[None, :], w_in.astype(BF16), conv_w, conv_b[None, :],
        _block_diag_tiles(w_q), _block_diag_tiles(w_k), _block_diag_tiles(w_v), gx, gi, pad(b_if[None, :]), seq=seq)
    hn = _ml_core(q, kT, v, gcol, grow, ln_w[None, :], batch=batch, seq=seq)
    return _ml_out(hn, xc, z, skip[None, :], w_out.astype(BF16), x2, in_place=in_place)


def _nsa_layer(x2, norm, w_in, b_gate, cmp_pe, cmp_w1, cmp_w2, w_out, rope_tabs, final_g, *, batch, seq, final):
    G, DK = NSA_GROUPS, NSA_DK
    hd = NSA_HEADS * DK
    base = 2 * hd + 6 * G * DK
    D = w_in.shape[0]
    src = jnp.arange(3 * NSA_HEADS)
    dst = (src // (3 * NSA_HPG)) * GATE_PAD + src % (3 * NSA_HPG)
    wg = jnp.zeros((D, LANES), F32).at[:, dst].set(w_in[:, base:])
    bg = jnp.zeros((1, LANES), F32).at[0, dst].set(b_gate)
    w = jnp.concatenate([w_in[:, :base], wg], axis=1).astype(BF16)
    qc, qr, kc, vc, ks, kw, vT, z, gate = _nsa_front(x2, norm[None, :], w, bg, *rope_tabs, batch=batch, seq=seq)
    half = CMP_STRIDE * DK
    w1ab = jnp.concatenate([cmp_w1[:, :half, :], cmp_w1[:, half:, :]], axis=2).astype(BF16)
    pe_flat = cmp_pe.reshape(2, 2, 1, half)
    pe2 = jnp.broadcast_to(pe_flat, (2, 2, 8, half)).reshape(2, 16, half).astype(BF16)
    w2p = jnp.zeros((2, cmp_w2.shape[1], LANES), F32).at[:, :, :DK].set(cmp_w2).astype(BF16)
    kcmp, vcmpT = _nsa_cmp(kc, vc, w1ab, pe2, w2p)
    o = _nsa_attn(qc, qr, kcmp, vcmpT, ks, kw, vT, gate, batch=batch, seq=seq)
    return _nsa_out(o, z, w_out.astype(BF16), x2, final_g[None, :], final=final)


def _rope_tables(seq):
    half = ROPE_DIM // 2
    pos = jnp.arange(seq, dtype=F32)
    inv_freq = ROPE_THETA ** (-jnp.arange(0, ROPE_DIM, 2, dtype=F32) / ROPE_DIM)
    ang = pos[:, None] * inv_freq[None, :]
    cos, sin = jnp.cos(ang), jnp.sin(ang)
    lane = jnp.arange(LANES) % NSA_DK
    idx = lane % half
    ctab = jnp.where(lane[None, :] < ROPE_DIM, cos[:, idx], 1.0)
    s1 = jnp.where((lane[None, :] >= half) & (lane[None, :] < ROPE_DIM), sin[:, idx], 0.0)
    s2 = jnp.where(lane[None, :] < half, -sin[:, idx], 0.0)
    return ctab, s1, s2


def kernel(x, ml_norm, ml_w_in, ml_conv_w, ml_conv_b, ml_w_q, ml_w_k, ml_w_v, ml_w_if, ml_b_if, ml_ln_w, ml_skip,
           ml_w_out, nsa_norm, nsa_w_in, nsa_b_gate, nsa_cmp_pe, nsa_cmp_w1, nsa_cmp_w2, nsa_w_out, final_norm):
    batch, seq, d_model = x.shape
    depth = ml_norm.shape[0] + nsa_norm.shape[0]
    assert seq % ML_CHUNK == 0 and seq % NSA_ROWS == 0 and seq % ML_ROWS == 0 and WINDOW % ATT_TILE == 0
    x2 = x.reshape(batch * seq, d_model)
    rope_tabs = _rope_tables(seq)
    for i in range(depth):
        j = i // 2
        if i % 2 == 0:
            x2 = _mlstm_layer(x2, ml_norm[j], ml_w_in[j], ml_conv_w[j], ml_conv_b[j], ml_w_q[j], ml_w_k[j],
                              ml_w_v[j], ml_w_if[j], ml_b_if[j], ml_ln_w[j], ml_skip[j], ml_w_out[j],
                              batch=batch, seq=seq, in_place=(i > 0))
        else:
            x2 = _nsa_layer(x2, nsa_norm[j], nsa_w_in[j], nsa_b_gate[j], nsa_cmp_pe[j], nsa_cmp_w1[j],
                            nsa_cmp_w2[j], nsa_w_out[j], rope_tabs, final_norm, batch=batch, seq=seq,
                            final=(i == depth - 1))
    if depth % 2 == 1:
        x2 = _final_norm(x2, final_norm[None, :])
    return x2.reshape(batch, seq, d_model)
```

```python
import functools

import jax
import jax.numpy as jnp
from jax import lax
from jax.experimental import pallas as pl
from jax.experimental.pallas import tpu as pltpu

F32 = jnp.float32
BF16 = jnp.bfloat16

RMS_EPS = 1e-6
LN_EPS = 1e-6
NEG = -1e30

ML_HEADS = 4
ML_CONV = 4
ML_QKV_BLK = 4
ML_CHUNK = 256
ML_ROWS = 512

NSA_HEADS = 16
NSA_GROUPS = 4
NSA_HPG = NSA_HEADS // NSA_GROUPS
NSA_DK = 64
ROPE_DIM = NSA_DK // 4
ROPE_THETA = 500000.0
CMP_BLOCK = 32
CMP_STRIDE = 16
SEL_BLOCK = 64
SEL_TOPK = 16
SEL_LOCAL = 2
SEL_FORCE = 1e9
WINDOW = 512
NSA_ROWS = 512
ATT_TILE = 256
GATE_PAD = 16
V_ROWS = NSA_DK + 16
LOG2E = 1.4426950408889634

LANES = 128
MXU_TILE = 256
VMEM_LIMIT = 56 * 1024 * 1024


def _dot(a, b):
    return jnp.dot(a, b, preferred_element_type=F32)


def _sigmoid(v):
    return 1.0 / (1.0 + jnp.exp(-v))


def _const_spec(shape):
    zeros = (0,) * len(shape)
    return pl.BlockSpec(shape, lambda *_: zeros, pipeline_mode=pl.Buffered(1))


def _params(sem, flags=None):
    return pltpu.CompilerParams(dimension_semantics=sem, vmem_limit_bytes=VMEM_LIMIT, flags=flags)


def _rmsnorm(x, g):
    ms = jnp.mean(x * x, axis=-1, keepdims=True)
    return (x * lax.rsqrt(ms + RMS_EPS)) * g


def _ml_front_kernel(x_ref, g_ref, win_ref, convw_ref, convb_ref, wq_ref, wk_ref, wv_ref, gx_ref, gi_ref, bif_ref,
                     q_ref, kT_ref, v_ref, xc_ref, z_ref, gcol_ref, grow_ref, xbuf, h_sc,
                     *, tm, tiles_per_seq, inner, q_scale):
    t = pl.program_id(0)
    h_sc[...] = _rmsnorm(x_ref[...], g_ref[...]).astype(BF16)

    @pl.when(t % tiles_per_seq == 0)
    def _():
        xbuf[:, 0:8, :] = jnp.zeros((inner // MXU_TILE, 8, MXU_TILE), F32)

    nchunk = inner // MXU_TILE

    def in_proj(c):
        sl = slice(c * MXU_TILE, (c + 1) * MXU_TILE)
        xbuf[c, 8:8 + tm, :] = _dot(h_sc[...], win_ref[:, sl])
        z_ref[:, sl] = _dot(h_sc[...], win_ref[:, inner + c * MXU_TILE:inner + (c + 1) * MXU_TILE]).astype(BF16)

    in_proj(0)
    gates = jnp.broadcast_to(bif_ref[...], (tm, LANES))
    for c in range(nchunk):
        sl = slice(c * MXU_TILE, (c + 1) * MXU_TILE)
        if c + 1 < nchunk:
            in_proj(c + 1)
        conv = convb_ref[:, sl]
        for tap in range(ML_CONV):
            lo = 8 - (ML_CONV - 1) + tap
            conv = conv + xbuf[c, lo:lo + tm, :] * convw_ref[tap:tap + 1, sl]
        xin_b = xbuf[c, 8:8 + tm, :].astype(BF16)
        xbuf[c, 0:8, :] = xbuf[c, tm:tm + 8, :]
        xc_b = (conv * _sigmoid(conv)).astype(BF16)
        xc_ref[:, sl] = xc_b
        q_ref[:, sl] = (_dot(xc_b, wq_ref[c]) * q_scale).astype(BF16)
        kT_ref[sl, :] = _dot(xc_b, wk_ref[c]).T.astype(BF16)
        v_ref[:, sl] = _dot(xin_b, wv_ref[c]).astype(BF16)
        gates = gates + _dot(xc_b, gx_ref[sl, :]) + _dot(xin_b, gi_ref[sl, :])

    logsig = -(jnp.maximum(-gates, 0.0) + jnp.log1p(jnp.exp(-jnp.abs(gates))))
    lane = lax.broadcasted_iota(jnp.int32, (tm, LANES), 1)
    gsel = jnp.where(lane < ML_HEADS, gates, logsig)
    gcol_ref[...] = gsel[:, 0:2 * ML_HEADS]
    grow_ref[...] = gsel.T[0:2 * ML_HEADS, :]


def _ml_front(x2, g, win, convw, convb, wq, wk, wv, gx, gi, bif, *, seq):
    T, D = x2.shape
    inner = win.shape[1] // 2
    tm = ML_ROWS
    nq = inner // MXU_TILE
    kern = functools.partial(_ml_front_kernel, tm=tm, tiles_per_seq=seq // tm, inner=inner,
                             q_scale=float((inner // ML_HEADS) ** -0.5))
    row = lambda w: pl.BlockSpec((tm, w), lambda t: (t, 0))
    bd = _const_spec((nq, MXU_TILE, MXU_TILE))
    return pl.pallas_call(
        kern,
        grid=(T // tm,),
        in_specs=[row(D), _const_spec((1, D)), _const_spec((D, 2 * inner)),
                  _const_spec((ML_CONV, inner)), _const_spec((1, inner)), bd, bd, bd,
                  _const_spec((inner, LANES)), _const_spec((inner, LANES)), _const_spec((1, LANES))],
        out_specs=[row(inner), pl.BlockSpec((inner, tm), lambda t: (0, t)), row(inner),
                   row(inner), row(inner), row(2 * ML_HEADS),
                   pl.BlockSpec((2 * ML_HEADS, tm), lambda t: (0, t))],
        out_shape=[jax.ShapeDtypeStruct((T, inner), BF16),
                   jax.ShapeDtypeStruct((inner, T), BF16), jax.ShapeDtypeStruct((T, inner), BF16),
                   jax.ShapeDtypeStruct((T, inner), BF16), jax.ShapeDtypeStruct((T, inner), BF16),
                   jax.ShapeDtypeStruct((T, 2 * ML_HEADS), F32), jax.ShapeDtypeStruct((2 * ML_HEADS, T), F32)],
        scratch_shapes=[pltpu.VMEM((nq, tm + 8, MXU_TILE), F32), pltpu.VMEM((tm, D), BF16)],
        compiler_params=_params(("arbitrary",)),
        name="ml_front",
    )(x2, g, win, convw, convb, wq, wk, wv, gx, gi, bif)


def _ml_core_kernel(q_ref, kT_ref, v_ref, gcol_ref, grow_ref, lnw_ref, o_ref, s_sc, n_sc, m_sc,
                    qk_sc, qs_sc, qn_sc, *, L, dh):
    c = pl.program_id(1)
    ones = jnp.ones((L, LANES), BF16)

    @pl.when(c == 0)
    def _():
        s_sc[...] = jnp.zeros_like(s_sc)
        n_sc[...] = jnp.zeros_like(n_sc)
        m_sc[...] = jnp.zeros_like(m_sc)

    gcol = gcol_ref[...]
    grow = grow_ref[...]
    row = lax.broadcasted_iota(jnp.int32, (L, L), 0)
    col = lax.broadcasted_iota(jnp.int32, (L, L), 1)
    causal = col <= row
    def query_matmuls(h):
        sl = slice(h * dh, (h + 1) * dh)
        q = q_ref[:, sl]
        qk_sc[h % 2] = _dot(q, kT_ref[sl, :])
        qs_sc[h % 2] = _dot(q, s_sc[h].astype(BF16))
        qn_sc[h % 2] = _dot(q, n_sc[h].astype(BF16))

    query_matmuls(0)
    for h in range(ML_HEADS):
        if h + 1 < ML_HEADS:
            query_matmuls(h + 1)
        sl = slice(h * dh, (h + 1) * dh)
        v = v_ref[:, sl]
        kT = kT_ref[sl, :]
        fc = gcol[:, ML_HEADS + h:ML_HEADS + h + 1]
        ir = grow[h:h + 1, :]
        fr = grow[ML_HEADS + h:ML_HEADS + h + 1, :]
        m_prev = m_sc[h][:, 0:1]
        b_col = jnp.sum(jnp.where(causal, fr, 0.0), axis=1, keepdims=True)
        b_row = jnp.sum(jnp.where(row <= col, fc, 0.0), axis=0, keepdims=True)
        dmat = jnp.where(causal, b_col - b_row + ir, -jnp.inf)
        inter = b_col + m_prev
        m_row = jnp.maximum(inter, jnp.max(dmat, axis=1, keepdims=True))
        w_intra = jnp.exp(dmat - m_row)
        w_inter = jnp.exp(inter - m_row)
        s = qk_sc[h % 2] * w_intra
        num = _dot(s.astype(BF16), v) + w_inter * qs_sc[h % 2]
        den = jnp.sum(s, axis=1, keepdims=True) + w_inter * qn_sc[h % 2][:, 0:1]
        hh = num * (1.0 / jnp.maximum(jnp.abs(den), jnp.exp(-m_row)))
        mu = jnp.mean(hh, axis=1, keepdims=True)
        cen = hh - mu
        var = jnp.mean(cen * cen, axis=1, keepdims=True)
        o_ref[:, sl] = ((cen * lax.rsqrt(var + LN_EPS)) * lnw_ref[:, sl]).astype(BF16)

        b_last = b_col[L - 1:L, :]
        a_row = b_last - b_row + ir
        m_new = jnp.maximum(b_last + m_prev, jnp.max(a_row, axis=1, keepdims=True))
        decay = jnp.exp(b_last + m_prev - m_new)
        kw = kT * jnp.exp(a_row - m_new).astype(BF16)
        s_sc[h] = decay * s_sc[h] + _dot(kw, v)
        n_sc[h] = decay * n_sc[h] + _dot(kw, ones)
        m_sc[h] = jnp.broadcast_to(m_new, (1, LANES))


def _ml_core(q, kT, v, gcol, grow, lnw, *, batch, seq):
    T, inner = q.shape
    L = ML_CHUNK
    nc = seq // L
    dh = inner // ML_HEADS
    kern = functools.partial(_ml_core_kernel, L=L, dh=dh)
    row = lambda w: pl.BlockSpec((L, w), lambda b, c: (b * nc + c, 0))
    colb = lambda r: pl.BlockSpec((r, L), lambda b, c: (0, b * nc + c))
    return pl.pallas_call(
        kern,
        grid=(batch, nc),
        in_specs=[row(inner), colb(inner), row(inner), row(2 * ML_HEADS), colb(2 * ML_HEADS),
                  _const_spec((1, inner))],
        out_specs=row(inner),
        out_shape=jax.ShapeDtypeStruct((T, inner), BF16),
        scratch_shapes=[pltpu.VMEM((ML_HEADS, dh, dh), F32), pltpu.VMEM((ML_HEADS, dh, LANES), F32),
                        pltpu.VMEM((ML_HEADS, 1, LANES), F32), pltpu.VMEM((2, L, L), F32),
                        pltpu.VMEM((2, L, dh), F32), pltpu.VMEM((2, L, LANES), F32)],
        compiler_params=_params(("arbitrary", "arbitrary")),
        name="ml_core",
    )(q, kT, v, gcol, grow, lnw)


def _ml_out_kernel(hn_ref, xc_ref, z_ref, skip_ref, wout_ref, x_ref, o_ref, u_sc):
    nk = hn_ref.shape[1] // MXU_TILE

    def gated(k):
        sl = slice(k * MXU_TILE, (k + 1) * MXU_TILE)
        z = z_ref[:, sl].astype(F32)
        u = (hn_ref[:, sl].astype(F32) + skip_ref[:, sl] * xc_ref[:, sl].astype(F32)) * (z * _sigmoid(z))
        u_sc[k % 2] = u.astype(BF16)

    gated(0)
    for k in range(nk):
        if k + 1 < nk:
            gated(k + 1)
        part = _dot(u_sc[k % 2], wout_ref[k * MXU_TILE:(k + 1) * MXU_TILE, :])
        o_ref[...] = (x_ref[...] if k == 0 else o_ref[...]) + part


def _ml_out(hn, xc, z, skip, wout, x2, *, in_place):
    T, inner = hn.shape
    D = x2.shape[1]
    tm = ML_ROWS
    row = lambda w: pl.BlockSpec((tm, w), lambda t: (t, 0))
    return pl.pallas_call(
        _ml_out_kernel,
        grid=(T // tm,),
        in_specs=[row(inner), row(inner), row(inner), _const_spec((1, inner)), _const_spec((inner, D)), row(D)],
        out_specs=row(D),
        out_shape=jax.ShapeDtypeStruct((T, D), F32),
        input_output_aliases={5: 0} if in_place else {},
        scratch_shapes=[pltpu.VMEM((2, tm, MXU_TILE), BF16)],
        compiler_params=_params(("arbitrary",)),
        name="ml_out",
    )(hn, xc, z, skip, wout, x2)


def _rope(a, cos, s1, s2):
    return a * cos + pltpu.roll(a, ROPE_DIM // 2, 1) * s1 + pltpu.roll(a, LANES - ROPE_DIM // 2, 1) * s2


def _nsa_front_kernel(x_ref, g_ref, w_ref, bg_ref, cos_ref, s1_ref, s2_ref,
                      qc_ref, qr_ref, kc_ref, vc_ref, ks_ref, kw_ref, vT_ref, z_ref, gate_ref,
                      h_sc, sec_sc, *, tm, hd, gd):
    h_sc[...] = _rmsnorm(x_ref[...], g_ref[...]).astype(BF16)
    cos, s1, s2 = cos_ref[...], s1_ref[...], s2_ref[...]
    scale = NSA_DK ** -0.5 * LOG2E
    G = NSA_GROUPS

    def slabs_of(sec, rope):
        slabs = [sec[:, c * LANES:(c + 1) * LANES] for c in range(sec.shape[1] // LANES)]
        return [_rope(a, cos, s1, s2) for a in slabs] if rope else slabs

    def store_q(c):
        def run(sec):
            for half, a in enumerate(slabs_of(sec, False)):
                sl = slice(c * MXU_TILE + half * LANES, c * MXU_TILE + (half + 1) * LANES)
                qc_ref[:, sl] = (a * scale).astype(BF16)
                qr_ref[:, sl] = (_rope(a, cos, s1, s2) * scale).astype(BF16)
        return run

    def store_rows(ref, rope):
        def run(sec):
            slabs = slabs_of(sec, rope)
            for g in range(G):
                lo = (g * NSA_DK) % LANES
                ref[0, g] = slabs[(g * NSA_DK) // LANES][:, lo:lo + NSA_DK].astype(ref.dtype)
        return run

    def store_cols(br):
        def run(sec):
            slabs = slabs_of(sec, False)
            ones = jnp.ones((V_ROWS - NSA_DK, ATT_TILE), BF16)
            for g in range(G):
                aT = slabs[(g * NSA_DK) // LANES].T
                lo = (g * NSA_DK) % LANES
                for j in range(tm // ATT_TILE):
                    vT_ref[0, g, br, j, 0:NSA_DK, :] = (
                        aT[lo:lo + NSA_DK, j * ATT_TILE:(j + 1) * ATT_TILE].astype(BF16))
                    vT_ref[0, g, br, j, NSA_DK:V_ROWS, :] = ones
        return run

    def store_z(c):
        def run(sec):
            z_ref[:, c * MXU_TILE:(c + 1) * MXU_TILE] = sec.astype(BF16)
        return run

    def store_gate(sec):
        gT = _sigmoid(sec + bg_ref[...]).T
        for g in range(G):
            gate_ref[g] = gT[g * GATE_PAD:(g + 1) * GATE_PAD, :]

    zoff = hd + 6 * gd
    sections = ([(c * MXU_TILE, MXU_TILE, store_q(c)) for c in range(hd // MXU_TILE)]
                + [(hd, gd, store_rows(kc_ref, False)), (hd + gd, gd, store_rows(vc_ref, False)),
                   (hd + 2 * gd, gd, store_rows(ks_ref, True)), (hd + 3 * gd, gd, store_cols(0)),
                   (hd + 4 * gd, gd, store_rows(kw_ref, True)), (hd + 5 * gd, gd, store_cols(1))]
                + [(zoff + c * MXU_TILE, MXU_TILE, store_z(c)) for c in range(hd // MXU_TILE)]
                + [(zoff + hd, LANES, store_gate)])

    def project(k):
        off, width, _ = sections[k]
        sec_sc[k % 2, :, 0:width] = _dot(h_sc[...], w_ref[:, off:off + width])

    project(0)
    for k, (_, width, consume) in enumerate(sections):
        if k + 1 < len(sections):
            project(k + 1)
        consume(sec_sc[k % 2, :, 0:width])


def _nsa_front(x2, g, w, bg, cos, s1, s2, *, batch, seq):
    T, D = x2.shape
    tm = NSA_ROWS
    tps = seq // tm
    G = NSA_GROUPS
    hd = NSA_HEADS * NSA_DK
    gd = G * NSA_DK
    ntk = seq // ATT_TILE
    kern = functools.partial(_nsa_front_kernel, tm=tm, hd=hd, gd=gd)
    row = lambda w_: pl.BlockSpec((tm, w_), lambda t: (t, 0))
    tab = pl.BlockSpec((tm, LANES), lambda t: (t % tps, 0))
    rows4 = pl.BlockSpec((1, G, tm, NSA_DK), lambda t: (t // tps, 0, t % tps, 0))
    cols6 = pl.BlockSpec((1, G, 2, tm // ATT_TILE, V_ROWS, ATT_TILE), lambda t: (t // tps, 0, 0, t % tps, 0, 0))
    rows_shape = jax.ShapeDtypeStruct((batch, G, seq, NSA_DK), BF16)
    rows_f32 = jax.ShapeDtypeStruct((batch, G, seq, NSA_DK), F32)
    cols_shape = jax.ShapeDtypeStruct((batch, G, 2, ntk, V_ROWS, ATT_TILE), BF16)
    return pl.pallas_call(
        kern,
        grid=(T // tm,),
        in_specs=[row(D), _const_spec((1, D)), _const_spec(w.shape), _const_spec((1, LANES)), tab, tab, tab],
        out_specs=[row(hd), row(hd), rows4, rows4, rows4, rows4, cols6, row(hd),
                   pl.BlockSpec((G, GATE_PAD, tm), lambda t: (0, 0, t))],
        out_shape=[jax.ShapeDtypeStruct((T, hd), BF16), jax.ShapeDtypeStruct((T, hd), BF16),
                   rows_f32, rows_f32, rows_shape, rows_shape, cols_shape,
                   jax.ShapeDtypeStruct((T, hd), BF16), jax.ShapeDtypeStruct((G, GATE_PAD, T), F32)],
        scratch_shapes=[pltpu.VMEM((tm, D), BF16), pltpu.VMEM((2, tm, MXU_TILE), F32)],
        compiler_params=_params(("arbitrary",)),
        name="nsa_front",
    )(x2, g, w, bg, cos, s1, s2)


def _nsa_cmp_kernel(kc_ref, vc_ref, w1_ref, pe_ref, w2_ref, kcmp_ref, vcmpT_ref, *, hid, n):
    def mlp(x_ref, i):
        ab = jnp.zeros((n, 2 * hid), F32)
        for r in range(CMP_STRIDE):
            xr = x_ref[0, 0, pl.ds(r, n, stride=CMP_STRIDE), :]
            ab = ab + _dot(xr.astype(BF16), w1_ref[i, r * NSA_DK:(r + 1) * NSA_DK, :])
        pb = _dot(pe_ref[i], w1_ref[i])
        bias = pb[0:1, 0:hid] + pb[8:9, hid:2 * hid]
        h1 = ab[:, 0:hid] + pltpu.roll(ab[:, hid:2 * hid], n - 1, 0) + bias
        return _dot((h1 * _sigmoid(h1)).astype(BF16), w2_ref[i])

    kcmp_ref[0, 0] = mlp(kc_ref, 0)[:, 0:NSA_DK].astype(BF16)
    vcmpT_ref[0, 0] = mlp(vc_ref, 1).T[0:NSA_DK, :].astype(BF16)


def _nsa_cmp(kc, vc, w1ab, pe2, w2p):
    B, G, seq, width = kc.shape
    n = seq // CMP_STRIDE
    hid = w1ab.shape[2] // 2
    blk = pl.BlockSpec((1, 1, seq, width), lambda b, g: (b, g, 0, 0))
    return pl.pallas_call(
        functools.partial(_nsa_cmp_kernel, hid=hid, n=n),
        grid=(B, G),
        in_specs=[blk, blk, _const_spec(w1ab.shape), _const_spec(pe2.shape), _const_spec(w2p.shape)],
        out_specs=[pl.BlockSpec((1, 1, n, NSA_DK), lambda b, g: (b, g, 0, 0)),
                   pl.BlockSpec((1, 1, NSA_DK, n), lambda b, g: (b, g, 0, 0))],
        out_shape=[jax.ShapeDtypeStruct((B, G, n, NSA_DK), BF16), jax.ShapeDtypeStruct((B, G, NSA_DK, n), BF16)],
        compiler_params=_params(("arbitrary", "arbitrary")),
        name="nsa_cmp",
    )(kc, vc, w1ab, pe2, w2p)


def _nsa_attn_kernel(qc_ref, qr_ref, kcmp_ref, vcmpT_ref, ks_ref, kw_ref, vT_ref, gate_ref, o_ref,
                     ksa_sc, qaug_sc, s_a, s_b, p_a, p_b, sw_sc, m_sc, alpha_sc, acc_sc, *, tq, seq):
    i = pl.program_id(2)
    H = NSA_HPG
    W = H * tq
    tk = tq
    nsel = seq // SEL_BLOCK
    ncmp = kcmp_ref.shape[2]
    nwin = WINDOW // tk + 1

    @pl.when(i == 0)
    def _():
        ksa_sc[:, 0:NSA_DK] = ks_ref[0, 0]
        kb = lax.broadcasted_iota(jnp.int32, (seq, LANES - NSA_DK), 0) // SEL_BLOCK
        nn = lax.broadcasted_iota(jnp.int32, (seq, LANES - NSA_DK), 1)
        ksa_sc[:, NSA_DK:LANES] = jnp.where(kb == nn, 1.0, 0.0).astype(BF16)

    def heads_on_lanes(q_tile):
        qT = q_tile.astype(F32).T
        return jnp.concatenate([qT[h * NSA_DK:(h + 1) * NSA_DK, :] for h in range(H)], axis=1)

    def key_rows(ref2d, j):
        return ref2d[pl.ds(pl.multiple_of(j * tk, tk), tk), :]

    qcT = heads_on_lanes(qc_ref[...]).astype(BF16)
    qrT = heads_on_lanes(qr_ref[...]).astype(BF16)
    delta = (lax.broadcasted_iota(jnp.int32, (tk, W), 0)
             - (lax.broadcasted_iota(jnp.int32, (tk, W), 1) & (tq - 1)))

    sc = _dot(kcmp_ref[0, 0], qcT)
    kw2 = kw_ref.at[0, 0]
    for d in range(nwin):
        sw_sc[d] = _dot(key_rows(kw2, jnp.maximum(i - d, 0)), qrT)

    cidx = lax.broadcasted_iota(jnp.int32, (ncmp, W), 0)
    qpos = i * tq + (lax.broadcasted_iota(jnp.int32, (ncmp, W), 1) & (tq - 1))
    sc = jnp.where(cidx * CMP_STRIDE + (CMP_BLOCK - 1) <= qpos, sc, NEG)
    p = jnp.exp2(sc - jnp.max(sc, axis=0, keepdims=True))
    p = p * (1.0 / jnp.sum(p, axis=0, keepdims=True))
    p = p * jnp.where(qpos[0:1, :] >= CMP_BLOCK - 1, 1.0, 0.0)
    ocmp = _dot(vcmpT_ref[0, 0], p.astype(BF16))

    psum = p[:, 0:tq]
    for h in range(1, H):
        psum = psum + p[:, h * tq:(h + 1) * tq]
    nn = lax.broadcasted_iota(jnp.int32, (nsel, ncmp), 0) * SEL_BLOCK
    cc = lax.broadcasted_iota(jnp.int32, (nsel, ncmp), 1) * CMP_STRIDE
    ov = jnp.maximum(jnp.minimum(cc + CMP_BLOCK, nn + SEL_BLOCK) - jnp.maximum(cc, nn), 0)
    ov = (ov.astype(F32) * (1.0 / CMP_STRIDE)).astype(BF16)
    p_hi = psum.astype(BF16)
    r1 = psum - p_hi.astype(F32)
    p_mid = r1.astype(BF16)
    p_lo = (r1 - p_mid.astype(F32)).astype(BF16)
    imp = _dot(ov, p_hi) + _dot(ov, p_mid) + _dot(ov, p_lo)

    nidx = lax.broadcasted_iota(jnp.int32, (nsel, tq), 0)
    qblk = (i * tq + lax.broadcasted_iota(jnp.int32, (nsel, tq), 1)) // SEL_BLOCK
    dist = qblk - nidx
    forced = (nidx == 0) | ((dist >= 0) & (dist < SEL_LOCAL))
    imp = jnp.where(forced, SEL_FORCE, jnp.where(dist >= 0, imp, -1.0))
    rank = jnp.zeros((nsel, tq), jnp.int32)
    for m in range(nsel):
        rm = imp[m:m + 1, :]
        before = (rm > imp) | ((rm == imp) & (nidx > m))
        rank = rank + before.astype(jnp.int32)
    sel = (rank < SEL_TOPK) & (dist >= 0)
    bias = jnp.where(sel, 0.0, NEG).astype(BF16)
    bias = jnp.concatenate([bias] * H, axis=1)
    qaug = jnp.concatenate([qrT, bias, jnp.zeros((LANES - NSA_DK - nsel, W), BF16)], axis=0)
    qaug_sc[...] = qaug

    s_b[...] = _dot(key_rows(ksa_sc, i), qaug)
    s_a[...] = _dot(key_rows(ksa_sc, 0), qaug)

    sw_sc[0] = jnp.where(delta <= 0, sw_sc[0], sw_sc[nwin - 1])
    m_w = jnp.max(sw_sc[0], axis=0, keepdims=True)
    for d in range(1, nwin - 1):
        m_w = jnp.maximum(m_w, jnp.where(i >= d, jnp.max(sw_sc[d], axis=0, keepdims=True), NEG))
    pt = jnp.exp2(sw_sc[0] - m_w)
    far_thr = jnp.where(i >= nwin - 1, 0, 1 << 20)
    acc_w = _dot(vT_ref[0, 0, 1, i], jnp.where(delta <= 0, pt, 0.0).astype(BF16))
    acc_w = acc_w + _dot(vT_ref[0, 0, 1, jnp.maximum(i - (nwin - 1), 0)],
                         jnp.where(delta > far_thr, pt, 0.0).astype(BF16))
    for d in range(1, nwin - 1):
        pt = jnp.exp2(sw_sc[d] - (m_w + jnp.where(i >= d, 0.0, -NEG)))
        acc_w = acc_w + _dot(vT_ref[0, 0, 1, jnp.maximum(i - d, 0)], pt.astype(BF16))
    owin = acc_w[0:NSA_DK, :] * (1.0 / acc_w[V_ROWS - 1:V_ROWS, :])

    s = jnp.where(delta <= 0, s_b[...], NEG)
    m0 = jnp.max(s, axis=0, keepdims=True)
    m_sc[...] = m0
    alpha_sc[...] = jnp.ones((1, W), F32)
    acc_sc[...] = jnp.zeros((V_ROWS, W), F32)
    p_b[...] = jnp.exp2(s - m0).astype(BF16)

    def value_stage(jm, p_ref):
        vT = vT_ref[0, 0, 0, jm]
        for h in range(H):
            ls = slice(h * tq, (h + 1) * tq)
            acc_sc[:, ls] = alpha_sc[:, ls] * acc_sc[:, ls] + _dot(vT, p_ref[:, ls])

    def stages(j, s_cur, p_cur, s_nxt, p_prv):
        vT = vT_ref[0, 0, 0, jnp.where(j == 0, i, j - 1)]
        k_nxt = key_rows(ksa_sc, jnp.minimum(j + 1, i - 1))
        for h in range(H):
            ls = slice(h * tq, (h + 1) * tq)
            acc_sc[:, ls] = alpha_sc[:, ls] * acc_sc[:, ls] + _dot(vT, p_prv[:, ls])
            m_old = m_sc[:, ls]
            m_new = jnp.maximum(m_old, jnp.max(s_cur[:, ls], axis=0, keepdims=True))
            m_sc[:, ls] = m_new
            alpha_sc[:, ls] = jnp.exp2(m_old - m_new)
            p_cur[:, ls] = jnp.exp2(s_cur[:, ls] - m_new).astype(BF16)
            s_nxt[:, ls] = _dot(k_nxt, qaug_sc[:, ls])

    def sel_body(j, carry):
        @pl.when((j & 1) == 0)
        def _():
            stages(j, s_a, p_a, s_b, p_b)

        @pl.when((j & 1) == 1)
        def _():
            stages(j, s_b, p_b, s_a, p_a)

        return carry

    lax.fori_loop(0, i, sel_body, 0)

    @pl.when((i & 1) == 0)
    def _():
        value_stage(jnp.where(i == 0, i, i - 1), p_b)

    @pl.when((i & 1) == 1)
    def _():
        value_stage(i - 1, p_a)

    acc = acc_sc[...]
    osel = acc[0:NSA_DK, :] * (1.0 / acc[V_ROWS - 1:V_ROWS, :])

    gate = gate_ref[0]
    outs = []
    for h in range(H):
        ls = slice(h * tq, (h + 1) * tq)
        outs.append(gate[3 * h:3 * h + 1, :] * ocmp[:, ls] + gate[3 * h + 1:3 * h + 2, :] * osel[:, ls]
                    + gate[3 * h + 2:3 * h + 3, :] * owin[:, ls])
    o_ref[...] = jnp.concatenate(outs, axis=0).T.astype(BF16)


def _nsa_attn(qc, qr, kcmp, vcmpT, ks, kw, vT, gate, *, batch, seq):
    T, hd = qc.shape
    G = NSA_GROUPS
    tq = ATT_TILE
    nq = seq // tq
    gw = NSA_HPG * NSA_DK
    W = NSA_HPG * tq
    ncmp = kcmp.shape[2]
    nwin = WINDOW // tq + 1
    qspec = pl.BlockSpec((tq, gw), lambda b, g, i: (b * nq + i, g))
    rows4 = pl.BlockSpec((1, 1, seq, NSA_DK), lambda b, g, i: (b, g, 0, 0))
    return pl.pallas_call(
        functools.partial(_nsa_attn_kernel, tq=tq, seq=seq),
        grid=(batch, G, nq),
        in_specs=[qspec, qspec,
                  pl.BlockSpec((1, 1, ncmp, NSA_DK), lambda b, g, i: (b, g, 0, 0)),
                  pl.BlockSpec((1, 1, NSA_DK, ncmp), lambda b, g, i: (b, g, 0, 0)),
                  rows4, rows4,
                  pl.BlockSpec((1, 1, 2, nq, V_ROWS, tq), lambda b, g, i: (b, g, 0, 0, 0, 0)),
                  pl.BlockSpec((1, GATE_PAD, tq), lambda b, g, i: (g, 0, b * nq + i))],
        out_specs=qspec,
        out_shape=jax.ShapeDtypeStruct((T, hd), BF16),
        scratch_shapes=[pltpu.VMEM((seq, LANES), BF16), pltpu.VMEM((LANES, W), BF16),
                        pltpu.VMEM((tq, W), F32), pltpu.VMEM((tq, W), F32),
                        pltpu.VMEM((tq, W), BF16), pltpu.VMEM((tq, W), BF16), pltpu.VMEM((nwin, tq, W), F32),
                        pltpu.VMEM((1, W), F32), pltpu.VMEM((1, W), F32), pltpu.VMEM((V_ROWS, W), F32)],
        compiler_params=_params(("arbitrary", "arbitrary", "arbitrary")),
        name="nsa_attn",
    )(qc, qr, kcmp, vcmpT, ks, kw, vT, gate)


def _nsa_out_kernel(o_ref, z_ref, wout_ref, x_ref, gf_ref, y_ref, u_sc, *, final):
    nk = o_ref.shape[1] // MXU_TILE

    def gated(k):
        sl = slice(k * MXU_TILE, (k + 1) * MXU_TILE)
        z = z_ref[:, sl].astype(F32)
        u_sc[k % 2] = (o_ref[:, sl].astype(F32) * (z * _sigmoid(z))).astype(BF16)

    gated(0)
    for k in range(nk):
        if k + 1 < nk:
            gated(k + 1)
        part = _dot(u_sc[k % 2], wout_ref[k * MXU_TILE:(k + 1) * MXU_TILE, :])
        y = (x_ref[...] if k == 0 else y_ref[...]) + part
        y_ref[...] = _rmsnorm(y, gf_ref[...]) if (final and k == nk - 1) else y


def _nsa_out(o, z, wout, x2, gf, *, final):
    T, hd = o.shape
    D = x2.shape[1]
    tm = NSA_ROWS
    row = lambda w: pl.BlockSpec((tm, w), lambda t: (t, 0))
    return pl.pallas_call(
        functools.partial(_nsa_out_kernel, final=final),
        grid=(T // tm,),
        in_specs=[row(hd), row(hd), _const_spec((hd, D)), row(D), _const_spec((1, D))],
        out_specs=row(D),
        out_shape=jax.ShapeDtypeStruct((T, D), F32),
        input_output_aliases={3: 0},
        scratch_shapes=[pltpu.VMEM((2, tm, MXU_TILE), BF16)],
        compiler_params=_params(("arbitrary",)),
        name="nsa_out",
    )(o, z, wout, x2, gf)


def _final_norm_kernel(x_ref, g_ref, y_ref):
    y_ref[...] = _rmsnorm(x_ref[...], g_ref[...])


def _final_norm(x2, gf):
    T, D = x2.shape
    tm = NSA_ROWS
    row = pl.BlockSpec((tm, D), lambda t: (t, 0))
    return pl.pallas_call(
        _final_norm_kernel, grid=(T // tm,), in_specs=[row, _const_spec((1, D))], out_specs=row,
        out_shape=jax.ShapeDtypeStruct((T, D), F32), compiler_params=_params(("arbitrary",)),
        name="final_norm",
    )(x2, gf)


def _block_diag_tiles(w):
    nblk = w.shape[0]
    per = MXU_TILE // ML_QKV_BLK
    rows = jnp.swapaxes(w, 1, 2).reshape(nblk // per, MXU_TILE, ML_QKV_BLK)
    tiled = jnp.tile(rows, (1, 1, per))
    blk = jnp.arange(MXU_TILE) // ML_QKV_BLK
    return jnp.where(blk[:, None] == blk[None, :], tiled, 0.0).astype(BF16)


def _fold_headwise(w, w_if_part):
    nblk = w.shape[0]
    folded = jnp.einsum('nij,nio->njo', w, w_if_part.reshape(nblk, ML_QKV_BLK, -1), precision='highest')
    return folded.reshape(nblk * ML_QKV_BLK, -1)


def _mlstm_layer(x2, norm, w_in, conv_w, conv_b, w_q, w_k, w_v, w_if, b_if, ln_w, skip, w_out,
                 *, batch, seq, in_place):
    inner = w_in.shape[1] // 2
    pad = lambda m: jnp.zeros((m.shape[0], LANES), F32).at[:, :2 * ML_HEADS].set(m)
    gx = pad(_fold_headwise(w_q, w_if[:inner]) + _fold_headwise(w_k, w_if[inner:2 * inner])).astype(BF16)
    gi = pad(_fold_headwise(w_v, w_if[2 * inner:])).astype(BF16)
    q, kT, v, xc, z, gcol, grow = _ml_front(
        x2, norm[None, :], w_in.astype(BF16), conv_w, conv_b[None, :],
        _block_diag_tiles(w_q), _block_diag_tiles(w_k), _block_diag_tiles(w_v), gx, gi, pad(b_if[None, :]), seq=seq)
    hn = _ml_core(q, kT, v, gcol, grow, ln_w[None, :], batch=batch, seq=seq)
    return _ml_out(hn, xc, z, skip[None, :], w_out.astype(BF16), x2, in_place=in_place)


def _nsa_layer(x2, norm, w_in, b_gate, cmp_pe, cmp_w1, cmp_w2, w_out, rope_tabs, final_g, *, batch, seq, final):
    G, DK = NSA_GROUPS, NSA_DK
    hd = NSA_HEADS * DK
    base = 2 * hd + 6 * G * DK
    D = w_in.shape[0]
    src = jnp.arange(3 * NSA_HEADS)
    dst = (src // (3 * NSA_HPG)) * GATE_PAD + src % (3 * NSA_HPG)
    wg = jnp.zeros((D, LANES), F32).at[:, dst].set(w_in[:, base:])
    bg = jnp.zeros((1, LANES), F32).at[0, dst].set(b_gate)
    w = jnp.concatenate([w_in[:, :base], wg], axis=1).astype(BF16)
    qc, qr, kc, vc, ks, kw, vT, z, gate = _nsa_front(x2, norm[None, :], w, bg, *rope_tabs, batch=batch, seq=seq)
    half = CMP_STRIDE * DK
    w1ab = jnp.concatenate([cmp_w1[:, :half, :], cmp_w1[:, half:, :]], axis=2).astype(BF16)
    pe_flat = cmp_pe.reshape(2, 2, 1, half)
    pe2 = jnp.broadcast_to(pe_flat, (2, 2, 8, half)).reshape(2, 16, half).astype(BF16)
    w2p = jnp.zeros((2, cmp_w2.shape[1], LANES), F32).at[:, :, :DK].set(cmp_w2).astype(BF16)
    kcmp, vcmpT = _nsa_cmp(kc, vc, w1ab, pe2, w2p)
    o = _nsa_attn(qc, qr, kcmp, vcmpT, ks, kw, vT, gate, batch=batch, seq=seq)
    return _nsa_out(o, z, w_out.astype(BF16), x2, final_g[None, :], final=final)


def _rope_tables(seq):
    half = ROPE_DIM // 2
    pos = jnp.arange(seq, dtype=F32)
    inv_freq = ROPE_THETA ** (-jnp.arange(0, ROPE_DIM, 2, dtype=F32) / ROPE_DIM)
    ang = pos[:, None] * inv_freq[None, :]
    cos, sin = jnp.cos(ang), jnp.sin(ang)
    lane = jnp.arange(LANES) % NSA_DK
    idx = lane % half
    ctab = jnp.where(lane[None, :] < ROPE_DIM, cos[:, idx], 1.0)
    s1 = jnp.where((lane[None, :] >= half) & (lane[None, :] < ROPE_DIM), sin[:, idx], 0.0)
    s2 = jnp.where(lane[None, :] < half, -sin[:, idx], 0.0)
    return ctab, s1, s2


def kernel(x, ml_norm, ml_w_in, ml_conv_w, ml_conv_b, ml_w_q, ml_w_k, ml_w_v, ml_w_if, ml_b_if, ml_ln_w, ml_skip,
           ml_w_out, nsa_norm, nsa_w_in, nsa_b_gate, nsa_cmp_pe, nsa_cmp_w1, nsa_cmp_w2, nsa_w_out, final_norm):
    batch, seq, d_model = x.shape
    depth = ml_norm.shape[0] + nsa_norm.shape[0]
    assert seq % ML_CHUNK == 0 and seq % NSA_ROWS == 0 and seq % ML_ROWS == 0 and WINDOW % ATT_TILE == 0
    x2 = x.reshape(batch * seq, d_model)
    rope_tabs = _rope_tables(seq)
    for i in range(depth):
        j = i // 2
        if i % 2 == 0:
            x2 = _mlstm_layer(x2, ml_norm[j], ml_w_in[j], ml_conv_w[j], ml_conv_b[j], ml_w_q[j], ml_w_k[j],
                              ml_w_v[j], ml_w_if[j], ml_b_if[j], ml_ln_w[j], ml_skip[j], ml_w_out[j],
                              batch=batch, seq=seq, in_place=(i > 0))
        else:
            x2 = _nsa_layer(x2, nsa_norm[j], nsa_w_in[j], nsa_b_gate[j], nsa_cmp_pe[j], nsa_cmp_w1[j],
                            nsa_cmp_w2[j], nsa_w_out[j], rope_tabs, final_norm, batch=batch, seq=seq,
                            final=(i == depth - 1))
    if depth % 2 == 1:
        x2 = _final_norm(x2, final_norm[None, :])
    return x2.reshape(batch, seq, d_model)
```

```python
import functools

import jax
import jax.numpy as jnp
from jax import lax
from jax.experimental import pallas as pl
from jax.experimental.pallas import tpu as pltpu

F32 = jnp.float32
BF16 = jnp.bfloat16

RMS_EPS = 1e-6
LN_EPS = 1e-6
NEG = -1e30

ML_HEADS = 4
ML_CONV = 4
ML_QKV_BLK = 4
ML_CHUNK = 256
ML_ROWS = 512

NSA_HEADS = 16
NSA_GROUPS = 4
NSA_HPG = NSA_HEADS // NSA_GROUPS
NSA_DK = 64
ROPE_DIM = NSA_DK // 4
ROPE_THETA = 500000.0
CMP_BLOCK = 32
CMP_STRIDE = 16
SEL_BLOCK = 64
SEL_TOPK = 16
SEL_LOCAL = 2
SEL_FORCE = 1e9
WINDOW = 512
NSA_ROWS = 512
ATT_TILE = 256
GATE_PAD = 16
V_ROWS = NSA_DK + 16
LOG2E = 1.4426950408889634

LANES = 128
MXU_TILE = 256
VMEM_LIMIT = 56 * 1024 * 1024


def _dot(a, b):
    return jnp.dot(a, b, preferred_element_type=F32)


def _sigmoid(v):
    return 1.0 / (1.0 + jnp.exp(-v))


def _const_spec(shape):
    zeros = (0,) * len(shape)
    return pl.BlockSpec(shape, lambda *_: zeros, pipeline_mode=pl.Buffered(1))


def _params(sem, flags=None):
    return pltpu.CompilerParams(dimension_semantics=sem, vmem_limit_bytes=VMEM_LIMIT, flags=flags)


def _rmsnorm(x, g):
    ms = jnp.mean(x * x, axis=-1, keepdims=True)
    return (x * lax.rsqrt(ms + RMS_EPS)) * g


def _ml_front_kernel(x_ref, g_ref, win_ref, convw_ref, convb_ref, wq_ref, wk_ref, wv_ref, gx_ref, gi_ref, bif_ref,
                     q_ref, kT_ref, v_ref, xc_ref, z_ref, gcol_ref, grow_ref, xbuf, h_sc,
                     *, tm, tiles_per_seq, inner, q_scale):
    t = pl.program_id(0)
    h_sc[...] = _rmsnorm(x_ref[...], g_ref[...]).astype(BF16)

    @pl.when(t % tiles_per_seq == 0)
    def _():
        xbuf[:, 0:8, :] = jnp.zeros((inner // MXU_TILE, 8, MXU_TILE), F32)

    nchunk = inner // MXU_TILE

    def in_proj(c):
        sl = slice(c * MXU_TILE, (c + 1) * MXU_TILE)
        xbuf[c, 8:8 + tm, :] = _dot(h_sc[...], win_ref[:, sl])
        z_ref[:, sl] = _dot(h_sc[...], win_ref[:, inner + c * MXU_TILE:inner + (c + 1) * MXU_TILE]).astype(BF16)

    in_proj(0)
    gates = jnp.broadcast_to(bif_ref[...], (tm, LANES))
    for c in range(nchunk):
        sl = slice(c * MXU_TILE, (c + 1) * MXU_TILE)
        if c + 1 < nchunk:
            in_proj(c + 1)
        conv = convb_ref[:, sl]
        for tap in range(ML_CONV):
            lo = 8 - (ML_CONV - 1) + tap
            conv = conv + xbuf[c, lo:lo + tm, :] * convw_ref[tap:tap + 1, sl]
        xin_b = xbuf[c, 8:8 + tm, :].astype(BF16)
        xbuf[c, 0:8, :] = xbuf[c, tm:tm + 8, :]
        xc_b = (conv * _sigmoid(conv)).astype(BF16)
        xc_ref[:, sl] = xc_b
        q_ref[:, sl] = (_dot(xc_b, wq_ref[c]) * q_scale).astype(BF16)
        kT_ref[sl, :] = _dot(xc_b, wk_ref[c]).T.astype(BF16)
        v_ref[:, sl] = _dot(xin_b, wv_ref[c]).astype(BF16)
        gates = gates + _dot(xc_b, gx_ref[sl, :]) + _dot(xin_b, gi_ref[sl, :])

    logsig = -(jnp.maximum(-gates, 0.0) + jnp.log1p(jnp.exp(-jnp.abs(gates))))
    lane = lax.broadcasted_iota(jnp.int32, (tm, LANES), 1)
    gsel = jnp.where(lane < ML_HEADS, gates, logsig)
    gcol_ref[...] = gsel[:, 0:2 * ML_HEADS]
    grow_ref[...] = gsel.T[0:2 * ML_HEADS, :]


def _ml_front(x2, g, win, convw, convb, wq, wk, wv, gx, gi, bif, *, seq):
    T, D = x2.shape
    inner = win.shape[1] // 2
    tm = ML_ROWS
    nq = inner // MXU_TILE
    kern = functools.partial(_ml_front_kernel, tm=tm, tiles_per_seq=seq // tm, inner=inner,
                             q_scale=float((inner // ML_HEADS) ** -0.5))
    row = lambda w: pl.BlockSpec((tm, w), lambda t: (t, 0))
    bd = _const_spec((nq, MXU_TILE, MXU_TILE))
    return pl.pallas_call(
        kern,
        grid=(T // tm,),
        in_specs=[row(D), _const_spec((1, D)), _const_spec((D, 2 * inner)),
                  _const_spec((ML_CONV, inner)), _const_spec((1, inner)), bd, bd, bd,
                  _const_spec((inner, LANES)), _const_spec((inner, LANES)), _const_spec((1, LANES))],
        out_specs=[row(inner), pl.BlockSpec((inner, tm), lambda t: (0, t)), row(inner),
                   row(inner), row(inner), row(2 * ML_HEADS),
                   pl.BlockSpec((2 * ML_HEADS, tm), lambda t: (0, t))],
        out_shape=[jax.ShapeDtypeStruct((T, inner), BF16),
                   jax.ShapeDtypeStruct((inner, T), BF16), jax.ShapeDtypeStruct((T, inner), BF16),
                   jax.ShapeDtypeStruct((T, inner), BF16), jax.ShapeDtypeStruct((T, inner), BF16),
                   jax.ShapeDtypeStruct((T, 2 * ML_HEADS), F32), jax.ShapeDtypeStruct((2 * ML_HEADS, T), F32)],
        scratch_shapes=[pltpu.VMEM((nq, tm + 8, MXU_TILE), F32), pltpu.VMEM((tm, D), BF16)],
        compiler_params=_params(("arbitrary",)),
        name="ml_front",
    )(x2, g, win, convw, convb, wq, wk, wv, gx, gi, bif)


def _ml_core_kernel(q_ref, kT_ref, v_ref, gcol_ref, grow_ref, lnw_ref, o_ref, s_sc, n_sc, m_sc,
                    qk_sc, qs_sc, qn_sc, *, L, dh):
    c = pl.program_id(1)
    ones = jnp.ones((L, LANES), BF16)

    @pl.when(c == 0)
    def _():
        s_sc[...] = jnp.zeros_like(s_sc)
        n_sc[...] = jnp.zeros_like(n_sc)
        m_sc[...] = jnp.zeros_like(m_sc)

    gcol = gcol_ref[...]
    grow = grow_ref[...]
    row = lax.broadcasted_iota(jnp.int32, (L, L), 0)
    col = lax.broadcasted_iota(jnp.int32, (L, L), 1)
    causal = col <= row
    def query_matmuls(h):
        sl = slice(h * dh, (h + 1) * dh)
        q = q_ref[:, sl]
        qk_sc[h % 2] = _dot(q, kT_ref[sl, :])
        qs_sc[h % 2] = _dot(q, s_sc[h].astype(BF16))
        qn_sc[h % 2] = _dot(q, n_sc[h].astype(BF16))

    query_matmuls(0)
    for h in range(ML_HEADS):
        if h + 1 < ML_HEADS:
            query_matmuls(h + 1)
        sl = slice(h * dh, (h + 1) * dh)
        v = v_ref[:, sl]
        kT = kT_ref[sl, :]
        fc = gcol[:, ML_HEADS + h:ML_HEADS + h + 1]
        ir = grow[h:h + 1, :]
        fr = grow[ML_HEADS + h:ML_HEADS + h + 1, :]
        m_prev = m_sc[h][:, 0:1]
        b_col = jnp.sum(jnp.where(causal, fr, 0.0), axis=1, keepdims=True)
        b_row = jnp.sum(jnp.where(row <= col, fc, 0.0), axis=0, keepdims=True)
        dmat = jnp.where(causal, b_col - b_row + ir, -jnp.inf)
        inter = b_col + m_prev
        m_row = jnp.maximum(inter, jnp.max(dmat, axis=1, keepdims=True))
        w_intra = jnp.exp(dmat - m_row)
        w_inter = jnp.exp(inter - m_row)
        s = qk_sc[h % 2] * w_intra
        num = _dot(s.astype(BF16), v) + w_inter * qs_sc[h % 2]
        den = jnp.sum(s, axis=1, keepdims=True) + w_inter * qn_sc[h % 2][:, 0:1]
        hh = num * (1.0 / jnp.maximum(jnp.abs(den), jnp.exp(-m_row)))
        mu = jnp.mean(hh, axis=1, keepdims=True)
        cen = hh - mu
        var = jnp.mean(cen * cen, axis=1, keepdims=True)
        o_ref[:, sl] = ((cen * lax.rsqrt(var + LN_EPS)) * lnw_ref[:, sl]).astype(BF16)

        b_last = b_col[L - 1:L, :]
        a_row = b_last - b_row + ir
        m_new = jnp.maximum(b_last + m_prev, jnp.max(a_row, axis=1, keepdims=True))
        decay = jnp.exp(b_last + m_prev - m_new)
        kw = kT * jnp.exp(a_row - m_new).astype(BF16)
        s_sc[h] = decay * s_sc[h] + _dot(kw, v)
        n_sc[h] = decay * n_sc[h] + _dot(kw, ones)
        m_sc[h] = jnp.broadcast_to(m_new, (1, LANES))


def _ml_core(q, kT, v, gcol, grow, lnw, *, batch, seq):
    T, inner = q.shape
    L = ML_CHUNK
    nc = seq // L
    dh = inner // ML_HEADS
    kern = functools.partial(_ml_core_kernel, L=L, dh=dh)
    row = lambda w: pl.BlockSpec((L, w), lambda b, c: (b * nc + c, 0))
    colb = lambda r: pl.BlockSpec((r, L), lambda b, c: (0, b * nc + c))
    return pl.pallas_call(
        kern,
        grid=(batch, nc),
        in_specs=[row(inner), colb(inner), row(inner), row(2 * ML_HEADS), colb(2 * ML_HEADS),
                  _const_spec((1, inner))],
        out_specs=row(inner),
        out_shape=jax.ShapeDtypeStruct((T, inner), BF16),
        scratch_shapes=[pltpu.VMEM((ML_HEADS, dh, dh), F32), pltpu.VMEM((ML_HEADS, dh, LANES), F32),
                        pltpu.VMEM((ML_HEADS, 1, LANES), F32), pltpu.VMEM((2, L, L), F32),
                        pltpu.VMEM((2, L, dh), F32), pltpu.VMEM((2, L, LANES), F32)],
        compiler_params=_params(("arbitrary", "arbitrary")),
        name="ml_core",
    )(q, kT, v, gcol, grow, lnw)


def _ml_out_kernel(hn_ref, xc_ref, z_ref, skip_ref, wout_ref, x_ref, o_ref, u_sc):
    nk = hn_ref.shape[1] // MXU_TILE

    def gated(k):
        sl = slice(k * MXU_TILE, (k + 1) * MXU_TILE)
        z = z_ref[:, sl].astype(F32)
        u = (hn_ref[:, sl].astype(F32) + skip_ref[:, sl] * xc_ref[:, sl].astype(F32)) * (z * _sigmoid(z))
        u_sc[k % 2] = u.astype(BF16)

    gated(0)
    for k in range(nk):
        if k + 1 < nk:
            gated(k + 1)
        part = _dot(u_sc[k % 2], wout_ref[k * MXU_TILE:(k + 1) * MXU_TILE, :])
        o_ref[...] = (x_ref[...] if k == 0 else o_ref[...]) + part


def _ml_out(hn, xc, z, skip, wout, x2, *, in_place):
    T, inner = hn.shape
    D = x2.shape[1]
    tm = ML_ROWS
    row = lambda w: pl.BlockSpec((tm, w), lambda t: (t, 0))
    return pl.pallas_call(
        _ml_out_kernel,
        grid=(T // tm,),
        in_specs=[row(inner), row(inner), row(inner), _const_spec((1, inner)), _const_spec((inner, D)), row(D)],
        out_specs=row(D),
        out_shape=jax.ShapeDtypeStruct((T, D), F32),
        input_output_aliases={5: 0} if in_place else {},
        scratch_shapes=[pltpu.VMEM((2, tm, MXU_TILE), BF16)],
        compiler_params=_params(("arbitrary",)),
        name="ml_out",
    )(hn, xc, z, skip, wout, x2)


def _rope(a, cos, s1, s2):
    return a * cos + pltpu.roll(a, ROPE_DIM // 2, 1) * s1 + pltpu.roll(a, LANES - ROPE_DIM // 2, 1) * s2


def _nsa_front_kernel(x_ref, g_ref, w_ref, bg_ref, cos_ref, s1_ref, s2_ref,
                      qc_ref, qr_ref, kc_ref, vc_ref, ks_ref, kw_ref, vT_ref, z_ref, gate_ref,
                      h_sc, sec_sc, *, tm, hd, gd):
    h_sc[...] = _rmsnorm(x_ref[...], g_ref[...]).astype(BF16)
    cos, s1, s2 = cos_ref[...], s1_ref[...], s2_ref[...]
    scale = NSA_DK ** -0.5 * LOG2E
    G = NSA_GROUPS

    def slabs_of(sec, rope):
        slabs = [sec[c] for c in range(MXU_TILE // LANES)]
        return [_rope(a, cos, s1, s2) for a in slabs] if rope else slabs

    def store_q(c):
        def run(sec):
            for half, a in enumerate(slabs_of(sec, False)):
                sl = slice(c * MXU_TILE + half * LANES, c * MXU_TILE + (half + 1) * LANES)
                qc_ref[:, sl] = (a * scale).astype(BF16)
                qr_ref[:, sl] = (_rope(a, cos, s1, s2) * scale).astype(BF16)
        return run

    def store_rows(ref, rope):
        def run(sec):
            slabs = slabs_of(sec, rope)
            for g in range(G):
                lo = (g * NSA_DK) % LANES
                ref[0, g] = slabs[(g * NSA_DK) // LANES][:, lo:lo + NSA_DK].astype(ref.dtype)
        return run

    def store_blocks(ref):
        def run(sec):
            for r in range(CMP_STRIDE):
                for c in range(MXU_TILE // LANES):
                    rows = sec[c, pl.ds(r, tm // CMP_STRIDE, stride=CMP_STRIDE), :]
                    for half in range(LANES // NSA_DK):
                        g = c * (LANES // NSA_DK) + half
                        ref[0, g, :, r * NSA_DK:(r + 1) * NSA_DK] = (
                            rows[:, half * NSA_DK:(half + 1) * NSA_DK].astype(BF16))
        return run

    def store_cols(br):
        def run(sec):
            slabs = slabs_of(sec, False)
            ones = jnp.ones((V_ROWS - NSA_DK, ATT_TILE), BF16)
            for g in range(G):
                aT = slabs[(g * NSA_DK) // LANES].T
                lo = (g * NSA_DK) % LANES
                for j in range(tm // ATT_TILE):
                    vT_ref[0, g, br, j, 0:NSA_DK, :] = (
                        aT[lo:lo + NSA_DK, j * ATT_TILE:(j + 1) * ATT_TILE].astype(BF16))
                    vT_ref[0, g, br, j, NSA_DK:V_ROWS, :] = ones
        return run

    def store_z(c):
        def run(sec):
            for half in range(MXU_TILE // LANES):
                z_ref[:, c * MXU_TILE + half * LANES:c * MXU_TILE + (half + 1) * LANES] = sec[half].astype(BF16)
        return run

    def store_gate(sec):
        gT = _sigmoid(sec[0] + bg_ref[...]).T
        for g in range(G):
            gate_ref[g] = gT[g * GATE_PAD:(g + 1) * GATE_PAD, :]

    zoff = hd + 6 * gd
    sections = ([(c * MXU_TILE, MXU_TILE, store_q(c)) for c in range(hd // MXU_TILE)]
                + [(hd, gd, store_blocks(kc_ref)), (hd + gd, gd, store_blocks(vc_ref)),
                   (hd + 2 * gd, gd, store_rows(ks_ref, True)), (hd + 3 * gd, gd, store_cols(0)),
                   (hd + 4 * gd, gd, store_rows(kw_ref, True)), (hd + 5 * gd, gd, store_cols(1))]
                + [(zoff + c * MXU_TILE, MXU_TILE, store_z(c)) for c in range(hd // MXU_TILE)]
                + [(zoff + hd, LANES, store_gate)])

    def project(k):
        off, width, _ = sections[k]
        res = _dot(h_sc[...], w_ref[:, off:off + width])
        for c in range(width // LANES):
            sec_sc[k % 2, c] = res[:, c * LANES:(c + 1) * LANES]

    project(0)
    for k, (_, _, consume) in enumerate(sections):
        if k + 1 < len(sections):
            project(k + 1)
        consume(sec_sc.at[k % 2])


def _nsa_front(x2, g, w, bg, cos, s1, s2, *, batch, seq):
    T, D = x2.shape
    tm = NSA_ROWS
    tps = seq // tm
    G = NSA_GROUPS
    hd = NSA_HEADS * NSA_DK
    gd = G * NSA_DK
    ntk = seq // ATT_TILE
    kern = functools.partial(_nsa_front_kernel, tm=tm, hd=hd, gd=gd)
    row = lambda w_: pl.BlockSpec((tm, w_), lambda t: (t, 0))
    tab = pl.BlockSpec((tm, LANES), lambda t: (t % tps, 0))
    rows4 = pl.BlockSpec((1, G, tm, NSA_DK), lambda t: (t // tps, 0, t % tps, 0))
    cols6 = pl.BlockSpec((1, G, 2, tm // ATT_TILE, V_ROWS, ATT_TILE), lambda t: (t // tps, 0, 0, t % tps, 0, 0))
    rows_shape = jax.ShapeDtypeStruct((batch, G, seq, NSA_DK), BF16)
    blk_w = CMP_STRIDE * NSA_DK
    blocks4 = pl.BlockSpec((1, G, tm // CMP_STRIDE, blk_w), lambda t: (t // tps, 0, t % tps, 0))
    blocks_shape = jax.ShapeDtypeStruct((batch, G, seq // CMP_STRIDE, blk_w), BF16)
    cols_shape = jax.ShapeDtypeStruct((batch, G, 2, ntk, V_ROWS, ATT_TILE), BF16)
    return pl.pallas_call(
        kern,
        grid=(T // tm,),
        in_specs=[row(D), _const_spec((1, D)), _const_spec(w.shape), _const_spec((1, LANES)), tab, tab, tab],
        out_specs=[row(hd), row(hd), blocks4, blocks4, rows4, rows4, cols6, row(hd),
                   pl.BlockSpec((G, GATE_PAD, tm), lambda t: (0, 0, t))],
        out_shape=[jax.ShapeDtypeStruct((T, hd), BF16), jax.ShapeDtypeStruct((T, hd), BF16),
                   blocks_shape, blocks_shape, rows_shape, rows_shape, cols_shape,
                   jax.ShapeDtypeStruct((T, hd), BF16), jax.ShapeDtypeStruct((G, GATE_PAD, T), F32)],
        scratch_shapes=[pltpu.VMEM((tm, D), BF16), pltpu.VMEM((2, MXU_TILE // LANES, tm, LANES), F32)],
        compiler_params=_params(("arbitrary",)),
        name="nsa_front",
    )(x2, g, w, bg, cos, s1, s2)


def _nsa_cmp_kernel(kc_ref, vc_ref, w1_ref, pe_ref, w2_ref, kcmp_ref, vcmpT_ref, *, hid, n):
    def mlp(x_ref, i):
        ab = _dot(x_ref[0, 0], w1_ref[i])
        pb = _dot(pe_ref[i], w1_ref[i])
        bias = pb[0:1, 0:hid] + pb[8:9, hid:2 * hid]
        h1 = ab[:, 0:hid] + pltpu.roll(ab[:, hid:2 * hid], n - 1, 0) + bias
        return _dot((h1 * _sigmoid(h1)).astype(BF16), w2_ref[i])

    kcmp_ref[0, 0] = mlp(kc_ref, 0)[:, 0:NSA_DK].astype(BF16)
    vcmpT_ref[0, 0] = mlp(vc_ref, 1).T[0:NSA_DK, :].astype(BF16)


def _nsa_cmp(kc, vc, w1ab, pe2, w2p):
    B, G, n, width = kc.shape
    hid = w1ab.shape[2] // 2
    blk = pl.BlockSpec((1, 1, n, width), lambda b, g: (b, g, 0, 0))
    return pl.pallas_call(
        functools.partial(_nsa_cmp_kernel, hid=hid, n=n),
        grid=(B, G),
        in_specs=[blk, blk, _const_spec(w1ab.shape), _const_spec(pe2.shape), _const_spec(w2p.shape)],
        out_specs=[pl.BlockSpec((1, 1, n, NSA_DK), lambda b, g: (b, g, 0, 0)),
                   pl.BlockSpec((1, 1, NSA_DK, n), lambda b, g: (b, g, 0, 0))],
        out_shape=[jax.ShapeDtypeStruct((B, G, n, NSA_DK), BF16), jax.ShapeDtypeStruct((B, G, NSA_DK, n), BF16)],
        compiler_params=_params(("arbitrary", "arbitrary")),
        name="nsa_cmp",
    )(kc, vc, w1ab, pe2, w2p)


def _nsa_attn_kernel(qc_ref, qr_ref, kcmp_ref, vcmpT_ref, ks_ref, kw_ref, vT_ref, gate_ref, o_ref,
                     ksa_sc, qaug_sc, s_a, s_b, p_a, p_b, sw_sc, m_sc, alpha_sc, acc_sc, *, tq, seq):
    i = pl.program_id(2)
    H = NSA_HPG
    W = H * tq
    tk = tq
    nsel = seq // SEL_BLOCK
    ncmp = kcmp_ref.shape[2]
    nwin = WINDOW // tk + 1

    @pl.when(i == 0)
    def _():
        ksa_sc[:, 0:NSA_DK] = ks_ref[0, 0]
        kb = lax.broadcasted_iota(jnp.int32, (seq, LANES - NSA_DK), 0) // SEL_BLOCK
        nn = lax.broadcasted_iota(jnp.int32, (seq, LANES - NSA_DK), 1)
        ksa_sc[:, NSA_DK:LANES] = jnp.where(kb == nn, 1.0, 0.0).astype(BF16)

    def heads_on_lanes(q_tile):
        qT = q_tile.astype(F32).T
        return jnp.concatenate([qT[h * NSA_DK:(h + 1) * NSA_DK, :] for h in range(H)], axis=1)

    def key_rows(ref2d, j):
        return ref2d[pl.ds(pl.multiple_of(j * tk, tk), tk), :]

    qcT = heads_on_lanes(qc_ref[...]).astype(BF16)
    qrT = heads_on_lanes(qr_ref[...]).astype(BF16)
    delta = (lax.broadcasted_iota(jnp.int32, (tk, W), 0)
             - (lax.broadcasted_iota(jnp.int32, (tk, W), 1) & (tq - 1)))

    sc = _dot(kcmp_ref[0, 0], qcT)
    kw2 = kw_ref.at[0, 0]
    for d in range(nwin):
        sw_sc[d] = _dot(key_rows(kw2, jnp.maximum(i - d, 0)), qrT)

    cidx = lax.broadcasted_iota(jnp.int32, (ncmp, W), 0)
    qpos = i * tq + (lax.broadcasted_iota(jnp.int32, (ncmp, W), 1) & (tq - 1))
    sc = jnp.where(cidx * CMP_STRIDE + (CMP_BLOCK - 1) <= qpos, sc, NEG)
    p = jnp.exp2(sc - jnp.max(sc, axis=0, keepdims=True))
    p = p * (1.0 / jnp.sum(p, axis=0, keepdims=True))
    p = p * jnp.where(qpos[0:1, :] >= CMP_BLOCK - 1, 1.0, 0.0)
    ocmp = _dot(vcmpT_ref[0, 0], p.astype(BF16))

    psum = p[:, 0:tq]
    for h in range(1, H):
        psum = psum + p[:, h * tq:(h + 1) * tq]
    nn = lax.broadcasted_iota(jnp.int32, (nsel, ncmp), 0) * SEL_BLOCK
    cc = lax.broadcasted_iota(jnp.int32, (nsel, ncmp), 1) * CMP_STRIDE
    ov = jnp.maximum(jnp.minimum(cc + CMP_BLOCK, nn + SEL_BLOCK) - jnp.maximum(cc, nn), 0)
    ov = (ov.astype(F32) * (1.0 / CMP_STRIDE)).astype(BF16)
    p_hi = psum.astype(BF16)
    r1 = psum - p_hi.astype(F32)
    p_mid = r1.astype(BF16)
    p_lo = (r1 - p_mid.astype(F32)).astype(BF16)
    imp = _dot(ov, p_hi) + _dot(ov, p_mid) + _dot(ov, p_lo)

    nidx = lax.broadcasted_iota(jnp.int32, (nsel, tq), 0)
    qblk = (i * tq + lax.broadcasted_iota(jnp.int32, (nsel, tq), 1)) // SEL_BLOCK
    dist = qblk - nidx
    forced = (nidx == 0) | ((dist >= 0) & (dist < SEL_LOCAL))
    imp = jnp.where(forced, SEL_FORCE, jnp.where(dist >= 0, imp, -1.0))
    rank = jnp.zeros((nsel, tq), jnp.int32)
    for m in range(nsel):
        rm = imp[m:m + 1, :]
        before = (rm > imp) | ((rm == imp) & (nidx > m))
        rank = rank + before.astype(jnp.int32)
    sel = (rank < SEL_TOPK) & (dist >= 0)
    bias = jnp.where(sel, 0.0, NEG).astype(BF16)
    bias = jnp.concatenate([bias] * H, axis=1)
    qaug = jnp.concatenate([qrT, bias, jnp.zeros((LANES - NSA_DK - nsel, W), BF16)], axis=0)
    qaug_sc[...] = qaug

    s_b[...] = _dot(key_rows(ksa_sc, i), qaug)
    s_a[...] = _dot(key_rows(ksa_sc, 0), qaug)

    sw_sc[0] = jnp.where(delta <= 0, sw_sc[0], sw_sc[nwin - 1])
    m_w = jnp.max(sw_sc[0], axis=0, keepdims=True)
    for d in range(1, nwin - 1):
        m_w = jnp.maximum(m_w, jnp.where(i >= d, jnp.max(sw_sc[d], axis=0, keepdims=True), NEG))
    pt = jnp.exp2(sw_sc[0] - m_w)
    far_thr = jnp.where(i >= nwin - 1, 0, 1 << 20)
    acc_w = _dot(vT_ref[0, 0, 1, i], jnp.where(delta <= 0, pt, 0.0).astype(BF16))
    acc_w = acc_w + _dot(vT_ref[0, 0, 1, jnp.maximum(i - (nwin - 1), 0)],
                         jnp.where(delta > far_thr, pt, 0.0).astype(BF16))
    for d in range(1, nwin - 1):
        pt = jnp.exp2(sw_sc[d] - (m_w + jnp.where(i >= d, 0.0, -NEG)))
        acc_w = acc_w + _dot(vT_ref[0, 0, 1, jnp.maximum(i - d, 0)], pt.astype(BF16))
    owin = acc_w[0:NSA_DK, :] * (1.0 / acc_w[V_ROWS - 1:V_ROWS, :])

    s = jnp.where(delta <= 0, s_b[...], NEG)
    m0 = jnp.max(s, axis=0, keepdims=True)
    m_sc[...] = m0
    alpha_sc[...] = jnp.ones((1, W), F32)
    acc_sc[...] = jnp.zeros((V_ROWS, W), F32)
    p_b[...] = jnp.exp2(s - m0).astype(BF16)

    def value_stage(jm, p_ref):
        vT = vT_ref[0, 0, 0, jm]
        for h in range(H):
            ls = slice(h * tq, (h + 1) * tq)
            acc_sc[:, ls] = alpha_sc[:, ls] * acc_sc[:, ls] + _dot(vT, p_ref[:, ls])

    def stages(j, s_cur, p_cur, s_nxt, p_prv):
        vT = vT_ref[0, 0, 0, jnp.where(j == 0, i, j - 1)]
        k_nxt = key_rows(ksa_sc, jnp.minimum(j + 1, i - 1))
        for h in range(H):
            ls = slice(h * tq, (h + 1) * tq)
            acc_sc[:, ls] = alpha_sc[:, ls] * acc_sc[:, ls] + _dot(vT, p_prv[:, ls])
            m_old = m_sc[:, ls]
            m_new = jnp.maximum(m_old, jnp.max(s_cur[:, ls], axis=0, keepdims=True))
            m_sc[:, ls] = m_new
            alpha_sc[:, ls] = jnp.exp2(m_old - m_new)
            p_cur[:, ls] = jnp.exp2(s_cur[:, ls] - m_new).astype(BF16)
            s_nxt[:, ls] = _dot(k_nxt, qaug_sc[:, ls])

    def sel_body(j, carry):
        @pl.when((j & 1) == 0)
        def _():
            stages(j, s_a, p_a, s_b, p_b)

        @pl.when((j & 1) == 1)
        def _():
            stages(j, s_b, p_b, s_a, p_a)

        return carry

    lax.fori_loop(0, i, sel_body, 0)

    @pl.when((i & 1) == 0)
    def _():
        value_stage(jnp.where(i == 0, i, i - 1), p_b)

    @pl.when((i & 1) == 1)
    def _():
        value_stage(i - 1, p_a)

    acc = acc_sc[...]
    osel = acc[0:NSA_DK, :] * (1.0 / acc[V_ROWS - 1:V_ROWS, :])

    gate = gate_ref[0]
    outs = []
    for h in range(H):
        ls = slice(h * tq, (h + 1) * tq)
        outs.append(gate[3 * h:3 * h + 1, :] * ocmp[:, ls] + gate[3 * h + 1:3 * h + 2, :] * osel[:, ls]
                    + gate[3 * h + 2:3 * h + 3, :] * owin[:, ls])
    o_ref[...] = jnp.concatenate(outs, axis=0).T.astype(BF16)


def _nsa_attn(qc, qr, kcmp, vcmpT, ks, kw, vT, gate, *, batch, seq):
    T, hd = qc.shape
    G = NSA_GROUPS
    tq = ATT_TILE
    nq = seq // tq
    gw = NSA_HPG * NSA_DK
    W = NSA_HPG * tq
    ncmp = kcmp.shape[2]
    nwin = WINDOW // tq + 1
    qspec = pl.BlockSpec((tq, gw), lambda b, g, i: (b * nq + i, g))
    rows4 = pl.BlockSpec((1, 1, seq, NSA_DK), lambda b, g, i: (b, g, 0, 0))
    return pl.pallas_call(
        functools.partial(_nsa_attn_kernel, tq=tq, seq=seq),
        grid=(batch, G, nq),
        in_specs=[qspec, qspec,
                  pl.BlockSpec((1, 1, ncmp, NSA_DK), lambda b, g, i: (b, g, 0, 0)),
                  pl.BlockSpec((1, 1, NSA_DK, ncmp), lambda b, g, i: (b, g, 0, 0)),
                  rows4, rows4,
                  pl.BlockSpec((1, 1, 2, nq, V_ROWS, tq), lambda b, g, i: (b, g, 0, 0, 0, 0)),
                  pl.BlockSpec((1, GATE_PAD, tq), lambda b, g, i: (g, 0, b * nq + i))],
        out_specs=qspec,
        out_shape=jax.ShapeDtypeStruct((T, hd), BF16),
        scratch_shapes=[pltpu.VMEM((seq, LANES), BF16), pltpu.VMEM((LANES, W), BF16),
                        pltpu.VMEM((tq, W), F32), pltpu.VMEM((tq, W), F32),
                        pltpu.VMEM((tq, W), BF16), pltpu.VMEM((tq, W), BF16), pltpu.VMEM((nwin, tq, W), F32),
                        pltpu.VMEM((1, W), F32), pltpu.VMEM((1, W), F32), pltpu.VMEM((V_ROWS, W), F32)],
        compiler_params=_params(("arbitrary", "arbitrary", "arbitrary")),
        name="nsa_attn",
    )(qc, qr, kcmp, vcmpT, ks, kw, vT, gate)


def _nsa_out_kernel(o_ref, z_ref, wout_ref, x_ref, gf_ref, y_ref, u_sc, *, final):
    nk = o_ref.shape[1] // MXU_TILE

    def gated(k):
        sl = slice(k * MXU_TILE, (k + 1) * MXU_TILE)
        z = z_ref[:, sl].astype(F32)
        u_sc[k % 2] = (o_ref[:, sl].astype(F32) * (z * _sigmoid(z))).astype(BF16)

    gated(0)
    for k in range(nk):
        if k + 1 < nk:
            gated(k + 1)
        part = _dot(u_sc[k % 2], wout_ref[k * MXU_TILE:(k + 1) * MXU_TILE, :])
        y = (x_ref[...] if k == 0 else y_ref[...]) + part
        y_ref[...] = _rmsnorm(y, gf_ref[...]) if (final and k == nk - 1) else y


def _nsa_out(o, z, wout, x2, gf, *, final):
    T, hd = o.shape
    D = x2.shape[1]
    tm = NSA_ROWS
    row = lambda w: pl.BlockSpec((tm, w), lambda t: (t, 0))
    return pl.pallas_call(
        functools.partial(_nsa_out_kernel, final=final),
        grid=(T // tm,),
        in_specs=[row(hd), row(hd), _const_spec((hd, D)), row(D), _const_spec((1, D))],
        out_specs=row(D),
        out_shape=jax.ShapeDtypeStruct((T, D), F32),
        input_output_aliases={3: 0},
        scratch_shapes=[pltpu.VMEM((2, tm, MXU_TILE), BF16)],
        compiler_params=_params(("arbitrary",)),
        name="nsa_out",
    )(o, z, wout, x2, gf)


def _final_norm_kernel(x_ref, g_ref, y_ref):
    y_ref[...] = _rmsnorm(x_ref[...], g_ref[...])


def _final_norm(x2, gf):
    T, D = x2.shape
    tm = NSA_ROWS
    row = pl.BlockSpec((tm, D), lambda t: (t, 0))
    return pl.pallas_call(
        _final_norm_kernel, grid=(T // tm,), in_specs=[row, _const_spec((1, D))], out_specs=row,
        out_shape=jax.ShapeDtypeStruct((T, D), F32), compiler_params=_params(("arbitrary",)),
        name="final_norm",
    )(x2, gf)


def _block_diag_tiles(w):
    nblk = w.shape[0]
    per = MXU_TILE // ML_QKV_BLK
    rows = jnp.swapaxes(w, 1, 2).reshape(nblk // per, MXU_TILE, ML_QKV_BLK)
    tiled = jnp.tile(rows, (1, 1, per))
    blk = jnp.arange(MXU_TILE) // ML_QKV_BLK
    return jnp.where(blk[:, None] == blk[None, :], tiled, 0.0).astype(BF16)


def _fold_headwise(w, w_if_part):
    nblk = w.shape[0]
    folded = jnp.einsum('nij,nio->njo', w, w_if_part.reshape(nblk, ML_QKV_BLK, -1), precision='highest')
    return folded.reshape(nblk * ML_QKV_BLK, -1)


def _mlstm_layer(x2, norm, w_in, conv_w, conv_b, w_q, w_k, w_v, w_if, b_if, ln_w, skip, w_out,
                 *, batch, seq, in_place):
    inner = w_in.shape[1] // 2
    pad = lambda m: jnp.zeros((m.shape[0], LANES), F32).at[:, :2 * ML_HEADS].set(m)
    gx = pad(_fold_headwise(w_q, w_if[:inner]) + _fold_headwise(w_k, w_if[inner:2 * inner])).astype(BF16)
    gi = pad(_fold_headwise(w_v, w_if[2 * inner:])).astype(BF16)
    q, kT, v, xc, z, gcol, grow = _ml_front(
        x2, norm[None, :], w_in.astype(BF16), conv_w, conv_b[None, :],
        _block_diag_tiles(w_q), _block_diag_tiles(w_k), _block_diag_tiles(w_v), gx, gi, pad(b_if[None, :]), seq=seq)
    hn = _ml_core(q, kT, v, gcol, grow, ln_w[None, :], batch=batch, seq=seq)
    return _ml_out(hn, xc, z, skip[None, :], w_out.astype(BF16), x2, in_place=in_place)


def _nsa_layer(x2, norm, w_in, b_gate, cmp_pe, cmp_w1, cmp_w2, w_out, rope_tabs, final_g, *, batch, seq, final):
    G, DK = NSA_GROUPS, NSA_DK
    hd = NSA_HEADS * DK
    base = 2 * hd + 6 * G * DK
    D = w_in.shape[0]
    src = jnp.arange(3 * NSA_HEADS)
    dst = (src // (3 * NSA_HPG)) * GATE_PAD + src % (3 * NSA_HPG)
    wg = jnp.zeros((D, LANES), F32).at[:, dst].set(w_in[:, base:])
    bg = jnp.zeros((1, LANES), F32).at[0, dst].set(b_gate)
    w = jnp.concatenate([w_in[:, :base], wg], axis=1).astype(BF16)
    qc, qr, kc, vc, ks, kw, vT, z, gate = _nsa_front(x2, norm[None, :], w, bg, *rope_tabs, batch=batch, seq=seq)
    half = CMP_STRIDE * DK
    w1ab = jnp.concatenate([cmp_w1[:, :half, :], cmp_w1[:, half:, :]], axis=2).astype(BF16)
    pe_flat = cmp_pe.reshape(2, 2, 1, half)
    pe2 = jnp.broadcast_to(pe_flat, (2, 2, 8, half)).reshape(2, 16, half).astype(BF16)
    w2p = jnp.zeros((2, cmp_w2.shape[1], LANES), F32).at[:, :, :DK].set(cmp_w2).astype(BF16)
    kcmp, vcmpT = _nsa_cmp(kc, vc, w1ab, pe2, w2p)
    o = _nsa_attn(qc, qr, kcmp, vcmpT, ks, kw, vT, gate, batch=batch, seq=seq)
    return _nsa_out(o, z, w_out.astype(BF16), x2, final_g[None, :], final=final)


def _rope_tables(seq):
    half = ROPE_DIM // 2
    pos = jnp.arange(seq, dtype=F32)
    inv_freq = ROPE_THETA ** (-jnp.arange(0, ROPE_DIM, 2, dtype=F32) / ROPE_DIM)
    ang = pos[:, None] * inv_freq[None, :]
    cos, sin = jnp.cos(ang), jnp.sin(ang)
    lane = jnp.arange(LANES) % NSA_DK
    idx = lane % half
    ctab = jnp.where(lane[None, :] < ROPE_DIM, cos[:, idx], 1.0)
    s1 = jnp.where((lane[None, :] >= half) & (lane[None, :] < ROPE_DIM), sin[:, idx], 0.0)
    s2 = jnp.where(lane[None, :] < half, -sin[:, idx], 0.0)
    return ctab, s1, s2


def kernel(x, ml_norm, ml_w_in, ml_conv_w, ml_conv_b, ml_w_q, ml_w_k, ml_w_v, ml_w_if, ml_b_if, ml_ln_w, ml_skip,
           ml_w_out, nsa_norm, nsa_w_in, nsa_b_gate, nsa_cmp_pe, nsa_cmp_w1, nsa_cmp_w2, nsa_w_out, final_norm):
    batch, seq, d_model = x.shape
    depth = ml_norm.shape[0] + nsa_norm.shape[0]
    assert seq % ML_CHUNK == 0 and seq % NSA_ROWS == 0 and seq % ML_ROWS == 0 and WINDOW % ATT_TILE == 0
    x2 = x.reshape(batch * seq, d_model)
    rope_tabs = _rope_tables(seq)
    for i in range(depth):
        j = i // 2
        if i % 2 == 0:
            x2 = _mlstm_layer(x2, ml_norm[j], ml_w_in[j], ml_conv_w[j], ml_conv_b[j], ml_w_q[j], ml_w_k[j],
                              ml_w_v[j], ml_w_if[j], ml_b_if[j], ml_ln_w[j], ml_skip[j], ml_w_out[j],
                              batch=batch, seq=seq, in_place=(i > 0))
        else:
            x2 = _nsa_layer(x2, nsa_norm[j], nsa_w_in[j], nsa_b_gate[j], nsa_cmp_pe[j], nsa_cmp_w1[j],
                            nsa_cmp_w2[j], nsa_w_out[j], rope_tabs, final_norm, batch=batch, seq=seq,
                            final=(i == depth - 1))
    if depth % 2 == 1:
        x2 = _final_norm(x2, final_norm[None, :])
    return x2.reshape(batch, seq, d_model)
```

```python
import functools

import jax
import jax.numpy as jnp
from jax import lax
from jax.experimental import pallas as pl
from jax.experimental.pallas import tpu as pltpu

F32 = jnp.float32
BF16 = jnp.bfloat16

RMS_EPS = 1e-6
LN_EPS = 1e-6
NEG = -1e30

ML_HEADS = 4
ML_CONV = 4
ML_QKV_BLK = 4
ML_CHUNK = 256
ML_ROWS = 512

NSA_HEADS = 16
NSA_GROUPS = 4
NSA_HPG = NSA_HEADS // NSA_GROUPS
NSA_DK = 64
ROPE_DIM = NSA_DK // 4
ROPE_THETA = 500000.0
CMP_BLOCK = 32
CMP_STRIDE = 16
SEL_BLOCK = 64
SEL_TOPK = 16
SEL_LOCAL = 2
SEL_FORCE = 1e9
WINDOW = 512
NSA_ROWS = 512
ATT_TILE = 256
GATE_PAD = 16
V_ROWS = NSA_DK + 16
LOG2E = 1.4426950408889634

LANES = 128
MXU_TILE = 256
VMEM_LIMIT = 56 * 1024 * 1024


def _dot(a, b):
    return jnp.dot(a, b, preferred_element_type=F32)


def _sigmoid(v):
    return 1.0 / (1.0 + jnp.exp(-v))


def _const_spec(shape):
    zeros = (0,) * len(shape)
    return pl.BlockSpec(shape, lambda *_: zeros, pipeline_mode=pl.Buffered(1))


def _params(sem, flags=None):
    return pltpu.CompilerParams(dimension_semantics=sem, vmem_limit_bytes=VMEM_LIMIT, flags=flags)


def _rmsnorm(x, g):
    ms = jnp.mean(x * x, axis=-1, keepdims=True)
    return (x * lax.rsqrt(ms + RMS_EPS)) * g


def _ml_front_kernel(x_ref, g_ref, win_ref, convw_ref, convb_ref, wq_ref, wk_ref, wv_ref, gx_ref, gi_ref, bif_ref,
                     q_ref, kT_ref, v_ref, xc_ref, z_ref, gcol_ref, grow_ref, xbuf, h_sc,
                     *, tm, tiles_per_seq, inner, q_scale):
    t = pl.program_id(0)
    h_sc[...] = _rmsnorm(x_ref[...], g_ref[...]).astype(BF16)

    @pl.when(t % tiles_per_seq == 0)
    def _():
        xbuf[:, 0:8, :] = jnp.zeros((inner // MXU_TILE, 8, MXU_TILE), F32)

    nchunk = inner // MXU_TILE

    def in_proj(c):
        sl = slice(c * MXU_TILE, (c + 1) * MXU_TILE)
        xbuf[c, 8:8 + tm, :] = _dot(h_sc[...], win_ref[:, sl])
        z_ref[:, sl] = _dot(h_sc[...], win_ref[:, inner + c * MXU_TILE:inner + (c + 1) * MXU_TILE]).astype(BF16)

    in_proj(0)
    gates = jnp.broadcast_to(bif_ref[...], (tm, LANES))
    for c in range(nchunk):
        sl = slice(c * MXU_TILE, (c + 1) * MXU_TILE)
        if c + 1 < nchunk:
            in_proj(c + 1)
        conv = convb_ref[:, sl]
        for tap in range(ML_CONV):
            lo = 8 - (ML_CONV - 1) + tap
            conv = conv + xbuf[c, lo:lo + tm, :] * convw_ref[tap:tap + 1, sl]
        xin_b = xbuf[c, 8:8 + tm, :].astype(BF16)
        xbuf[c, 0:8, :] = xbuf[c, tm:tm + 8, :]
        xc_b = (conv * _sigmoid(conv)).astype(BF16)
        xc_ref[:, sl] = xc_b
        q_ref[:, sl] = (_dot(xc_b, wq_ref[c]) * q_scale).astype(BF16)
        kT_ref[sl, :] = _dot(xc_b, wk_ref[c]).T.astype(BF16)
        v_ref[:, sl] = _dot(xin_b, wv_ref[c]).astype(BF16)
        gates = gates + _dot(xc_b, gx_ref[sl, :]) + _dot(xin_b, gi_ref[sl, :])

    logsig = -(jnp.maximum(-gates, 0.0) + jnp.log1p(jnp.exp(-jnp.abs(gates))))
    lane = lax.broadcasted_iota(jnp.int32, (tm, LANES), 1)
    gsel = jnp.where(lane < ML_HEADS, gates, logsig)
    gcol_ref[...] = gsel[:, 0:2 * ML_HEADS]
    grow_ref[...] = gsel.T[0:2 * ML_HEADS, :]


def _ml_front(x2, g, win, convw, convb, wq, wk, wv, gx, gi, bif, *, seq):
    T, D = x2.shape
    inner = win.shape[1] // 2
    tm = ML_ROWS
    nq = inner // MXU_TILE
    kern = functools.partial(_ml_front_kernel, tm=tm, tiles_per_seq=seq // tm, inner=inner,
                             q_scale=float((inner // ML_HEADS) ** -0.5))
    row = lambda w: pl.BlockSpec((tm, w), lambda t: (t, 0))
    bd = _const_spec((nq, MXU_TILE, MXU_TILE))
    return pl.pallas_call(
        kern,
        grid=(T // tm,),
        in_specs=[row(D), _const_spec((1, D)), _const_spec((D, 2 * inner)),
                  _const_spec((ML_CONV, inner)), _const_spec((1, inner)), bd, bd, bd,
                  _const_spec((inner, LANES)), _const_spec((inner, LANES)), _const_spec((1, LANES))],
        out_specs=[row(inner), pl.BlockSpec((inner, tm), lambda t: (0, t)), row(inner),
                   row(inner), row(inner), row(2 * ML_HEADS),
                   pl.BlockSpec((2 * ML_HEADS, tm), lambda t: (0, t))],
        out_shape=[jax.ShapeDtypeStruct((T, inner), BF16),
                   jax.ShapeDtypeStruct((inner, T), BF16), jax.ShapeDtypeStruct((T, inner), BF16),
                   jax.ShapeDtypeStruct((T, inner), BF16), jax.ShapeDtypeStruct((T, inner), BF16),
                   jax.ShapeDtypeStruct((T, 2 * ML_HEADS), F32), jax.ShapeDtypeStruct((2 * ML_HEADS, T), F32)],
        scratch_shapes=[pltpu.VMEM((nq, tm + 8, MXU_TILE), F32), pltpu.VMEM((tm, D), BF16)],
        compiler_params=_params(("arbitrary",)),
        name="ml_front",
    )(x2, g, win, convw, convb, wq, wk, wv, gx, gi, bif)


def _ml_core_kernel(q_ref, kT_ref, v_ref, gcol_ref, grow_ref, lnw_ref, o_ref, s_sc, n_sc, m_sc,
                    qk_sc, qs_sc, qn_sc, *, L, dh):
    c = pl.program_id(1)
    ones = jnp.ones((L, LANES), BF16)

    @pl.when(c == 0)
    def _():
        s_sc[...] = jnp.zeros_like(s_sc)
        n_sc[...] = jnp.zeros_like(n_sc)
        m_sc[...] = jnp.zeros_like(m_sc)

    gcol = gcol_ref[...]
    grow = grow_ref[...]
    row = lax.broadcasted_iota(jnp.int32, (L, L), 0)
    col = lax.broadcasted_iota(jnp.int32, (L, L), 1)
    causal = col <= row
    def query_matmuls(h):
        sl = slice(h * dh, (h + 1) * dh)
        q = q_ref[:, sl]
        qk_sc[h % 2] = _dot(q, kT_ref[sl, :])
        qs_sc[h % 2] = _dot(q, s_sc[h].astype(BF16))
        qn_sc[h % 2] = _dot(q, n_sc[h].astype(BF16))

    query_matmuls(0)
    for h in range(ML_HEADS):
        if h + 1 < ML_HEADS:
            query_matmuls(h + 1)
        sl = slice(h * dh, (h + 1) * dh)
        v = v_ref[:, sl]
        kT = kT_ref[sl, :]
        fc = gcol[:, ML_HEADS + h:ML_HEADS + h + 1]
        ir = grow[h:h + 1, :]
        fr = grow[ML_HEADS + h:ML_HEADS + h + 1, :]
        m_prev = m_sc[h][:, 0:1]
        b_col = jnp.sum(jnp.where(causal, fr, 0.0), axis=1, keepdims=True)
        b_row = jnp.sum(jnp.where(row <= col, fc, 0.0), axis=0, keepdims=True)
        dmat = jnp.where(causal, b_col - b_row + ir, -jnp.inf)
        inter = b_col + m_prev
        m_row = jnp.maximum(inter, jnp.max(dmat, axis=1, keepdims=True))
        w_intra = jnp.exp(dmat - m_row)
        w_inter = jnp.exp(inter - m_row)
        s = qk_sc[h % 2] * w_intra
        num = _dot(s.astype(BF16), v) + w_inter * qs_sc[h % 2]
        den = jnp.sum(s, axis=1, keepdims=True) + w_inter * qn_sc[h % 2][:, 0:1]
        hh = num * (1.0 / jnp.maximum(jnp.abs(den), jnp.exp(-m_row)))
        mu = jnp.mean(hh, axis=1, keepdims=True)
        cen = hh - mu
        var = jnp.mean(cen * cen, axis=1, keepdims=True)
        o_ref[:, sl] = ((cen * lax.rsqrt(var + LN_EPS)) * lnw_ref[:, sl]).astype(BF16)

        b_last = b_col[L - 1:L, :]
        a_row = b_last - b_row + ir
        m_new = jnp.maximum(b_last + m_prev, jnp.max(a_row, axis=1, keepdims=True))
        decay = jnp.exp(b_last + m_prev - m_new)
        kw = kT * jnp.exp(a_row - m_new).astype(BF16)
        s_sc[h] = decay * s_sc[h] + _dot(kw, v)
        n_sc[h] = decay * n_sc[h] + _dot(kw, ones)
        m_sc[h] = jnp.broadcast_to(m_new, (1, LANES))


def _ml_core(q, kT, v, gcol, grow, lnw, *, batch, seq):
    T, inner = q.shape
    L = ML_CHUNK
    nc = seq // L
    dh = inner // ML_HEADS
    kern = functools.partial(_ml_core_kernel, L=L, dh=dh)
    row = lambda w: pl.BlockSpec((L, w), lambda b, c: (b * nc + c, 0))
    colb = lambda r: pl.BlockSpec((r, L), lambda b, c: (0, b * nc + c))
    return pl.pallas_call(
        kern,
        grid=(batch, nc),
        in_specs=[row(inner), colb(inner), row(inner), row(2 * ML_HEADS), colb(2 * ML_HEADS),
                  _const_spec((1, inner))],
        out_specs=row(inner),
        out_shape=jax.ShapeDtypeStruct((T, inner), BF16),
        scratch_shapes=[pltpu.VMEM((ML_HEADS, dh, dh), F32), pltpu.VMEM((ML_HEADS, dh, LANES), F32),
                        pltpu.VMEM((ML_HEADS, 1, LANES), F32), pltpu.VMEM((2, L, L), F32),
                        pltpu.VMEM((2, L, dh), F32), pltpu.VMEM((2, L, LANES), F32)],
        compiler_params=_params(("arbitrary", "arbitrary")),
        name="ml_core",
    )(q, kT, v, gcol, grow, lnw)


def _ml_out_kernel(hn_ref, xc_ref, z_ref, skip_ref, wout_ref, x_ref, o_ref, u_sc):
    nk = hn_ref.shape[1] // MXU_TILE

    def gated(k):
        sl = slice(k * MXU_TILE, (k + 1) * MXU_TILE)
        z = z_ref[:, sl].astype(F32)
        u = (hn_ref[:, sl].astype(F32) + skip_ref[:, sl] * xc_ref[:, sl].astype(F32)) * (z * _sigmoid(z))
        u_sc[k % 2] = u.astype(BF16)

    gated(0)
    for k in range(nk):
        if k + 1 < nk:
            gated(k + 1)
        part = _dot(u_sc[k % 2], wout_ref[k * MXU_TILE:(k + 1) * MXU_TILE, :])
        o_ref[...] = (x_ref[...] if k == 0 else o_ref[...]) + part


def _ml_out(hn, xc, z, skip, wout, x2, *, in_place):
    T, inner = hn.shape
    D = x2.shape[1]
    tm = ML_ROWS
    row = lambda w: pl.BlockSpec((tm, w), lambda t: (t, 0))
    return pl.pallas_call(
        _ml_out_kernel,
        grid=(T // tm,),
        in_specs=[row(inner), row(inner), row(inner), _const_spec((1, inner)), _const_spec((inner, D)), row(D)],
        out_specs=row(D),
        out_shape=jax.ShapeDtypeStruct((T, D), F32),
        input_output_aliases={5: 0} if in_place else {},
        scratch_shapes=[pltpu.VMEM((2, tm, MXU_TILE), BF16)],
        compiler_params=_params(("arbitrary",)),
        name="ml_out",
    )(hn, xc, z, skip, wout, x2)


def _rope(a, cos, s1, s2):
    return a * cos + pltpu.roll(a, ROPE_DIM // 2, 1) * s1 + pltpu.roll(a, LANES - ROPE_DIM // 2, 1) * s2


def _nsa_front_kernel(x_ref, g_ref, w_ref, bg_ref, cos_ref, s1_ref, s2_ref,
                      qc_ref, qr_ref, kc_ref, vc_ref, ks_ref, kw_ref, vT_ref, z_ref, gate_ref,
                      h_sc, sec_sc, *, tm, hd, gd):
    h_sc[...] = _rmsnorm(x_ref[...], g_ref[...]).astype(BF16)
    cos, s1, s2 = cos_ref[...], s1_ref[...], s2_ref[...]
    scale = NSA_DK ** -0.5 * LOG2E
    G = NSA_GROUPS

    def slabs_of(sec, rope):
        slabs = [sec[c] for c in range(MXU_TILE // LANES)]
        return [_rope(a, cos, s1, s2) for a in slabs] if rope else slabs

    def store_q(c):
        def run(sec):
            for half, a in enumerate(slabs_of(sec, False)):
                sl = slice(c * MXU_TILE + half * LANES, c * MXU_TILE + (half + 1) * LANES)
                qc_ref[:, sl] = (a * scale).astype(BF16)
                qr_ref[:, sl] = (_rope(a, cos, s1, s2) * scale).astype(BF16)
        return run

    def store_rows(ref, rope):
        def run(sec):
            slabs = slabs_of(sec, rope)
            for g in range(G):
                lo = (g * NSA_DK) % LANES
                ref[0, g] = slabs[(g * NSA_DK) // LANES][:, lo:lo + NSA_DK].astype(ref.dtype)
        return run

    def store_blocks(ref):
        def run(sec):
            for r in range(CMP_STRIDE):
                for c in range(MXU_TILE // LANES):
                    rows = sec[c, pl.ds(r, tm // CMP_STRIDE, stride=CMP_STRIDE), :]
                    for half in range(LANES // NSA_DK):
                        g = c * (LANES // NSA_DK) + half
                        ref[0, g, :, r * NSA_DK:(r + 1) * NSA_DK] = (
                            rows[:, half * NSA_DK:(half + 1) * NSA_DK].astype(BF16))
        return run

    def store_cols(br):
        def run(sec):
            slabs = slabs_of(sec, False)
            ones = jnp.ones((V_ROWS - NSA_DK, ATT_TILE), BF16)
            for g in range(G):
                aT = slabs[(g * NSA_DK) // LANES].T
                lo = (g * NSA_DK) % LANES
                for j in range(tm // ATT_TILE):
                    vT_ref[0, g, br, j, 0:NSA_DK, :] = (
                        aT[lo:lo + NSA_DK, j * ATT_TILE:(j + 1) * ATT_TILE].astype(BF16))
                    vT_ref[0, g, br, j, NSA_DK:V_ROWS, :] = ones
        return run

    def store_z(c):
        def run(sec):
            for half in range(MXU_TILE // LANES):
                z_ref[:, c * MXU_TILE + half * LANES:c * MXU_TILE + (half + 1) * LANES] = sec[half].astype(BF16)
        return run

    def store_gate(sec):
        gT = _sigmoid(sec[0] + bg_ref[...]).T
        for g in range(G):
            for j in range(tm // ATT_TILE):
                gate_ref[g, j] = gT[g * GATE_PAD:(g + 1) * GATE_PAD, j * ATT_TILE:(j + 1) * ATT_TILE]

    zoff = hd + 6 * gd
    sections = ([(c * MXU_TILE, MXU_TILE, store_q(c)) for c in range(hd // MXU_TILE)]
                + [(hd, gd, store_blocks(kc_ref)), (hd + gd, gd, store_blocks(vc_ref)),
                   (hd + 2 * gd, gd, store_rows(ks_ref, True)), (hd + 3 * gd, gd, store_cols(0)),
                   (hd + 4 * gd, gd, store_rows(kw_ref, True)), (hd + 5 * gd, gd, store_cols(1))]
                + [(zoff + c * MXU_TILE, MXU_TILE, store_z(c)) for c in range(hd // MXU_TILE)]
                + [(zoff + hd, LANES, store_gate)])

    def project(k):
        off, width, _ = sections[k]
        res = _dot(h_sc[...], w_ref[:, off:off + width])
        for c in range(width // LANES):
            sec_sc[k % 2, c] = res[:, c * LANES:(c + 1) * LANES]

    project(0)
    for k, (_, _, consume) in enumerate(sections):
        if k + 1 < len(sections):
            project(k + 1)
        consume(sec_sc.at[k % 2])


def _nsa_front(x2, g, w, bg, cos, s1, s2, *, batch, seq):
    T, D = x2.shape
    tm = NSA_ROWS
    tps = seq // tm
    G = NSA_GROUPS
    hd = NSA_HEADS * NSA_DK
    gd = G * NSA_DK
    ntk = seq // ATT_TILE
    kern = functools.partial(_nsa_front_kernel, tm=tm, hd=hd, gd=gd)
    row = lambda w_: pl.BlockSpec((tm, w_), lambda t: (t, 0))
    tab = pl.BlockSpec((tm, LANES), lambda t: (t % tps, 0))
    rows4 = pl.BlockSpec((1, G, tm, NSA_DK), lambda t: (t // tps, 0, t % tps, 0))
    cols6 = pl.BlockSpec((1, G, 2, tm // ATT_TILE, V_ROWS, ATT_TILE), lambda t: (t // tps, 0, 0, t % tps, 0, 0))
    rows_shape = jax.ShapeDtypeStruct((batch, G, seq, NSA_DK), BF16)
    blk_w = CMP_STRIDE * NSA_DK
    blocks4 = pl.BlockSpec((1, G, tm // CMP_STRIDE, blk_w), lambda t: (t // tps, 0, t % tps, 0))
    blocks_shape = jax.ShapeDtypeStruct((batch, G, seq // CMP_STRIDE, blk_w), BF16)
    cols_shape = jax.ShapeDtypeStruct((batch, G, 2, ntk, V_ROWS, ATT_TILE), BF16)
    return pl.pallas_call(
        kern,
        grid=(T // tm,),
        in_specs=[row(D), _const_spec((1, D)), _const_spec(w.shape), _const_spec((1, LANES)), tab, tab, tab],
        out_specs=[row(hd), row(hd), blocks4, blocks4, rows4, rows4, cols6, row(hd),
                   pl.BlockSpec((G, tm // ATT_TILE, GATE_PAD, ATT_TILE), lambda t: (0, t, 0, 0))],
        out_shape=[jax.ShapeDtypeStruct((T, hd), BF16), jax.ShapeDtypeStruct((T, hd), BF16),
                   blocks_shape, blocks_shape, rows_shape, rows_shape, cols_shape,
                   jax.ShapeDtypeStruct((T, hd), BF16),
                   jax.ShapeDtypeStruct((G, T // ATT_TILE, GATE_PAD, ATT_TILE), F32)],
        scratch_shapes=[pltpu.VMEM((tm, D), BF16), pltpu.VMEM((2, MXU_TILE // LANES, tm, LANES), F32)],
        compiler_params=_params(("arbitrary",)),
        name="nsa_front",
    )(x2, g, w, bg, cos, s1, s2)


def _nsa_cmp_kernel(kc_ref, vc_ref, w1_ref, pe_ref, w2_ref, kcmp_ref, vcmpT_ref, *, hid, n):
    def mlp(x_ref, i):
        ab = _dot(x_ref[0, 0], w1_ref[i])
        pb = _dot(pe_ref[i], w1_ref[i])
        bias = pb[0:1, 0:hid] + pb[8:9, hid:2 * hid]
        h1 = ab[:, 0:hid] + pltpu.roll(ab[:, hid:2 * hid], n - 1, 0) + bias
        return _dot((h1 * _sigmoid(h1)).astype(BF16), w2_ref[i])

    kcmp_ref[0, 0] = mlp(kc_ref, 0)[:, 0:NSA_DK].astype(BF16)
    vcmpT_ref[0, 0] = mlp(vc_ref, 1).T[0:NSA_DK, :].astype(BF16)


def _nsa_cmp(kc, vc, w1ab, pe2, w2p):
    B, G, n, width = kc.shape
    hid = w1ab.shape[2] // 2
    blk = pl.BlockSpec((1, 1, n, width), lambda b, g: (b, g, 0, 0))
    return pl.pallas_call(
        functools.partial(_nsa_cmp_kernel, hid=hid, n=n),
        grid=(B, G),
        in_specs=[blk, blk, _const_spec(w1ab.shape), _const_spec(pe2.shape), _const_spec(w2p.shape)],
        out_specs=[pl.BlockSpec((1, 1, n, NSA_DK), lambda b, g: (b, g, 0, 0)),
                   pl.BlockSpec((1, 1, NSA_DK, n), lambda b, g: (b, g, 0, 0))],
        out_shape=[jax.ShapeDtypeStruct((B, G, n, NSA_DK), BF16), jax.ShapeDtypeStruct((B, G, NSA_DK, n), BF16)],
        compiler_params=_params(("arbitrary", "arbitrary")),
        name="nsa_cmp",
    )(kc, vc, w1ab, pe2, w2p)


def _nsa_attn_kernel(qc_ref, qr_ref, kcmp_ref, vcmpT_ref, ks_ref, kw_ref, vT_ref, gate_ref, o_ref,
                     ksa_sc, qaug_sc, s_a, s_b, p_a, p_b, sw_sc, m_sc, alpha_sc, acc_sc, *, tq, seq, i):
    H = NSA_HPG
    W = H * tq
    tk = tq
    nsel = seq // SEL_BLOCK
    ncmp = kcmp_ref.shape[2]
    nwin = WINDOW // tk + 1

    @pl.when(i == 0)
    def _():
        ksa_sc[:, 0:NSA_DK] = ks_ref[0, 0]
        kb = lax.broadcasted_iota(jnp.int32, (seq, LANES - NSA_DK), 0) // SEL_BLOCK
        nn = lax.broadcasted_iota(jnp.int32, (seq, LANES - NSA_DK), 1)
        ksa_sc[:, NSA_DK:LANES] = jnp.where(kb == nn, 1.0, 0.0).astype(BF16)

    def heads_on_lanes(q_tile):
        qT = q_tile.astype(F32).T
        return jnp.concatenate([qT[h * NSA_DK:(h + 1) * NSA_DK, :] for h in range(H)], axis=1)

    def key_rows(ref2d, j):
        return ref2d[pl.ds(pl.multiple_of(j * tk, tk), tk), :]

    qcT = heads_on_lanes(qc_ref[...]).astype(BF16)
    qrT = heads_on_lanes(qr_ref[...]).astype(BF16)
    delta = (lax.broadcasted_iota(jnp.int32, (tk, W), 0)
             - (lax.broadcasted_iota(jnp.int32, (tk, W), 1) & (tq - 1)))

    sc = _dot(kcmp_ref[0, 0], qcT)
    kw2 = kw_ref.at[0, 0]
    for d in range(nwin):
        sw_sc[d] = _dot(key_rows(kw2, jnp.maximum(i - d, 0)), qrT)

    cidx = lax.broadcasted_iota(jnp.int32, (ncmp, W), 0)
    qpos = i * tq + (lax.broadcasted_iota(jnp.int32, (ncmp, W), 1) & (tq - 1))
    sc = jnp.where(cidx * CMP_STRIDE + (CMP_BLOCK - 1) <= qpos, sc, NEG)
    p = jnp.exp2(sc - jnp.max(sc, axis=0, keepdims=True))
    p = p * (1.0 / jnp.sum(p, axis=0, keepdims=True))
    p = p * jnp.where(qpos[0:1, :] >= CMP_BLOCK - 1, 1.0, 0.0)
    ocmp = _dot(vcmpT_ref[0, 0], p.astype(BF16))

    psum = p[:, 0:tq]
    for h in range(1, H):
        psum = psum + p[:, h * tq:(h + 1) * tq]
    nn = lax.broadcasted_iota(jnp.int32, (nsel, ncmp), 0) * SEL_BLOCK
    cc = lax.broadcasted_iota(jnp.int32, (nsel, ncmp), 1) * CMP_STRIDE
    ov = jnp.maximum(jnp.minimum(cc + CMP_BLOCK, nn + SEL_BLOCK) - jnp.maximum(cc, nn), 0)
    ov = (ov.astype(F32) * (1.0 / CMP_STRIDE)).astype(BF16)
    p_hi = psum.astype(BF16)
    r1 = psum - p_hi.astype(F32)
    p_mid = r1.astype(BF16)
    p_lo = (r1 - p_mid.astype(F32)).astype(BF16)
    imp = _dot(ov, p_hi) + _dot(ov, p_mid) + _dot(ov, p_lo)

    nidx = lax.broadcasted_iota(jnp.int32, (nsel, tq), 0)
    qblk = (i * tq + lax.broadcasted_iota(jnp.int32, (nsel, tq), 1)) // SEL_BLOCK
    dist = qblk - nidx
    forced = (nidx == 0) | ((dist >= 0) & (dist < SEL_LOCAL))
    imp = jnp.where(forced, SEL_FORCE, jnp.where(dist >= 0, imp, -1.0))
    rank = jnp.zeros((nsel, tq), jnp.int32)
    for m in range(nsel):
        rm = imp[m:m + 1, :]
        before = (rm > imp) | ((rm == imp) & (nidx > m))
        rank = rank + before.astype(jnp.int32)
    sel = (rank < SEL_TOPK) & (dist >= 0)
    bias = jnp.where(sel, 0.0, NEG).astype(BF16)
    bias = jnp.concatenate([bias] * H, axis=1)
    qaug = jnp.concatenate([qrT, bias, jnp.zeros((LANES - NSA_DK - nsel, W), BF16)], axis=0)
    qaug_sc[...] = qaug

    s_b[...] = _dot(key_rows(ksa_sc, i), qaug)
    s_a[...] = _dot(key_rows(ksa_sc, 0), qaug)

    sw_sc[0] = jnp.where(delta <= 0, sw_sc[0], sw_sc[nwin - 1])
    m_w = jnp.max(sw_sc[0], axis=0, keepdims=True)
    for d in range(1, nwin - 1):
        m_w = jnp.maximum(m_w, jnp.where(i >= d, jnp.max(sw_sc[d], axis=0, keepdims=True), NEG))
    pt = jnp.exp2(sw_sc[0] - m_w)
    far_thr = jnp.where(i >= nwin - 1, 0, 1 << 20)
    acc_w = _dot(vT_ref[0, 0, 1, i], jnp.where(delta <= 0, pt, 0.0).astype(BF16))
    acc_w = acc_w + _dot(vT_ref[0, 0, 1, jnp.maximum(i - (nwin - 1), 0)],
                         jnp.where(delta > far_thr, pt, 0.0).astype(BF16))
    for d in range(1, nwin - 1):
        pt = jnp.exp2(sw_sc[d] - (m_w + jnp.where(i >= d, 0.0, -NEG)))
        acc_w = acc_w + _dot(vT_ref[0, 0, 1, jnp.maximum(i - d, 0)], pt.astype(BF16))
    owin = acc_w[0:NSA_DK, :] * (1.0 / acc_w[V_ROWS - 1:V_ROWS, :])

    s = jnp.where(delta <= 0, s_b[...], NEG)
    m0 = jnp.max(s, axis=0, keepdims=True)
    m_sc[...] = m0
    alpha_sc[...] = jnp.ones((1, W), F32)
    acc_sc[...] = jnp.zeros((V_ROWS, W), F32)
    p_b[...] = jnp.exp2(s - m0).astype(BF16)

    def value_stage(jm, p_ref):
        vT = vT_ref[0, 0, 0, jm]
        for h in range(H):
            ls = slice(h * tq, (h + 1) * tq)
            acc_sc[:, ls] = alpha_sc[:, ls] * acc_sc[:, ls] + _dot(vT, p_ref[:, ls])

    def stages(j, s_cur, p_cur, s_nxt, p_prv):
        vT = vT_ref[0, 0, 0, jnp.where(j == 0, i, j - 1)]
        k_nxt = key_rows(ksa_sc, jnp.minimum(j + 1, i - 1))
        for h in range(H):
            ls = slice(h * tq, (h + 1) * tq)
            acc_sc[:, ls] = alpha_sc[:, ls] * acc_sc[:, ls] + _dot(vT, p_prv[:, ls])
            m_old = m_sc[:, ls]
            m_new = jnp.maximum(m_old, jnp.max(s_cur[:, ls], axis=0, keepdims=True))
            m_sc[:, ls] = m_new
            alpha_sc[:, ls] = jnp.exp2(m_old - m_new)
            p_cur[:, ls] = jnp.exp2(s_cur[:, ls] - m_new).astype(BF16)
            s_nxt[:, ls] = _dot(k_nxt, qaug_sc[:, ls])

    def sel_body(j, carry):
        @pl.when((j & 1) == 0)
        def _():
            stages(j, s_a, p_a, s_b, p_b)

        @pl.when((j & 1) == 1)
        def _():
            stages(j, s_b, p_b, s_a, p_a)

        return carry

    lax.fori_loop(0, i, sel_body, 0)

    @pl.when((i & 1) == 0)
    def _():
        value_stage(jnp.where(i == 0, i, i - 1), p_b)

    @pl.when((i & 1) == 1)
    def _():
        value_stage(i - 1, p_a)

    acc = acc_sc[...]
    osel = acc[0:NSA_DK, :] * (1.0 / acc[V_ROWS - 1:V_ROWS, :])

    gate = gate_ref[0]
    outs = []
    for h in range(H):
        ls = slice(h * tq, (h + 1) * tq)
        outs.append(gate[3 * h:3 * h + 1, :] * ocmp[:, ls] + gate[3 * h + 1:3 * h + 2, :] * osel[:, ls]
                    + gate[3 * h + 2:3 * h + 3, :] * owin[:, ls])
    o_ref[...] = jnp.concatenate(outs, axis=0).T.astype(BF16)


def _nsa_attn_group_kernel(qc_ref, qr_ref, kcmp_ref, vcmpT_ref, ks_ref, kw_ref, vT_ref, gate_ref, o_ref, *scratch,
                           tq, seq):
    def tile(i, carry):
        rows = pl.ds(pl.multiple_of(i * tq, tq), tq)
        _nsa_attn_kernel(qc_ref.at[rows, :], qr_ref.at[rows, :], kcmp_ref, vcmpT_ref, ks_ref, kw_ref, vT_ref,
                         gate_ref.at[:, i], o_ref.at[rows, :], *scratch, tq=tq, seq=seq, i=i)
        return carry

    lax.fori_loop(0, seq // tq, tile, 0)


def _nsa_attn(qc, qr, kcmp, vcmpT, ks, kw, vT, gate, *, batch, seq):
    T, hd = qc.shape
    G = NSA_GROUPS
    tq = ATT_TILE
    nq = seq // tq
    gw = NSA_HPG * NSA_DK
    W = NSA_HPG * tq
    ncmp = kcmp.shape[2]
    nwin = WINDOW // tq + 1
    qspec = pl.BlockSpec((seq, gw), lambda b, g: (b, g))
    rows4 = pl.BlockSpec((1, 1, seq, NSA_DK), lambda b, g: (b, g, 0, 0))
    return pl.pallas_call(
        functools.partial(_nsa_attn_group_kernel, tq=tq, seq=seq),
        grid=(batch, G),
        in_specs=[qspec, qspec,
                  pl.BlockSpec((1, 1, ncmp, NSA_DK), lambda b, g: (b, g, 0, 0)),
                  pl.BlockSpec((1, 1, NSA_DK, ncmp), lambda b, g: (b, g, 0, 0)),
                  rows4, rows4,
                  pl.BlockSpec((1, 1, 2, nq, V_ROWS, tq), lambda b, g: (b, g, 0, 0, 0, 0)),
                  pl.BlockSpec((1, nq, GATE_PAD, tq), lambda b, g: (g, b, 0, 0))],
        out_specs=qspec,
        out_shape=jax.ShapeDtypeStruct((T, hd), BF16),
        scratch_shapes=[pltpu.VMEM((seq, LANES), BF16), pltpu.VMEM((LANES, W), BF16),
                        pltpu.VMEM((tq, W), F32), pltpu.VMEM((tq, W), F32),
                        pltpu.VMEM((tq, W), BF16), pltpu.VMEM((tq, W), BF16), pltpu.VMEM((nwin, tq, W), F32),
                        pltpu.VMEM((1, W), F32), pltpu.VMEM((1, W), F32), pltpu.VMEM((V_ROWS, W), F32)],
        compiler_params=_params(("arbitrary", "arbitrary")),
        name="nsa_attn",
    )(qc, qr, kcmp, vcmpT, ks, kw, vT, gate)


def _nsa_out_kernel(o_ref, z_ref, wout_ref, x_ref, gf_ref, y_ref, u_sc, *, final):
    nk = o_ref.shape[1] // MXU_TILE

    def gated(k):
        sl = slice(k * MXU_TILE, (k + 1) * MXU_TILE)
        z = z_ref[:, sl].astype(F32)
        u_sc[k % 2] = (o_ref[:, sl].astype(F32) * (z * _sigmoid(z))).astype(BF16)

    gated(0)
    for k in range(nk):
        if k + 1 < nk:
            gated(k + 1)
        part = _dot(u_sc[k % 2], wout_ref[k * MXU_TILE:(k + 1) * MXU_TILE, :])
        y = (x_ref[...] if k == 0 else y_ref[...]) + part
        y_ref[...] = _rmsnorm(y, gf_ref[...]) if (final and k == nk - 1) else y


def _nsa_out(o, z, wout, x2, gf, *, final):
    T, hd = o.shape
    D = x2.shape[1]
    tm = NSA_ROWS
    row = lambda w: pl.BlockSpec((tm, w), lambda t: (t, 0))
    return pl.pallas_call(
        functools.partial(_nsa_out_kernel, final=final),
        grid=(T // tm,),
        in_specs=[row(hd), row(hd), _const_spec((hd, D)), row(D), _const_spec((1, D))],
        out_specs=row(D),
        out_shape=jax.ShapeDtypeStruct((T, D), F32),
        input_output_aliases={3: 0},
        scratch_shapes=[pltpu.VMEM((2, tm, MXU_TILE), BF16)],
        compiler_params=_params(("arbitrary",)),
        name="nsa_out",
    )(o, z, wout, x2, gf)


def _final_norm_kernel(x_ref, g_ref, y_ref):
    y_ref[...] = _rmsnorm(x_ref[...], g_ref[...])


def _final_norm(x2, gf):
    T, D = x2.shape
    tm = NSA_ROWS
    row = pl.BlockSpec((tm, D), lambda t: (t, 0))
    return pl.pallas_call(
        _final_norm_kernel, grid=(T // tm,), in_specs=[row, _const_spec((1, D))], out_specs=row,
        out_shape=jax.ShapeDtypeStruct((T, D), F32), compiler_params=_params(("arbitrary",)),
        name="final_norm",
    )(x2, gf)


def _block_diag_tiles(w):
    nblk = w.shape[0]
    per = MXU_TILE // ML_QKV_BLK
    rows = jnp.swapaxes(w, 1, 2).reshape(nblk // per, MXU_TILE, ML_QKV_BLK)
    tiled = jnp.tile(rows, (1, 1, per))
    blk = jnp.arange(MXU_TILE) // ML_QKV_BLK
    return jnp.where(blk[:, None] == blk[None, :], tiled, 0.0).astype(BF16)


def _fold_headwise(w, w_if_part):
    nblk = w.shape[0]
    folded = jnp.einsum('nij,nio->njo', w, w_if_part.reshape(nblk, ML_QKV_BLK, -1), precision='highest')
    return folded.reshape(nblk * ML_QKV_BLK, -1)


def _mlstm_layer(x2, norm, w_in, conv_w, conv_b, w_q, w_k, w_v, w_if, b_if, ln_w, skip, w_out,
                 *, batch, seq, in_place):
    inner = w_in.shape[1] // 2
    pad = lambda m: jnp.zeros((m.shape[0], LANES), F32).at[:, :2 * ML_HEADS].set(m)
    gx = pad(_fold_headwise(w_q, w_if[:inner]) + _fold_headwise(w_k, w_if[inner:2 * inner])).astype(BF16)
    gi = pad(_fold_headwise(w_v, w_if[2 * inner:])).astype(BF16)
    q, kT, v, xc, z, gcol, grow = _ml_front(
        x2, norm[None, :], w_in.astype(BF16), conv_w, conv_b[None, :],
        _block_diag_tiles(w_q), _block_diag_tiles(w_k), _block_diag_tiles(w_v), gx, gi, pad(b_if[None, :]), seq=seq)
    hn = _ml_core(q, kT, v, gcol, grow, ln_w[None, :], batch=batch, seq=seq)
    return _ml_out(hn, xc, z, skip[None, :], w_out.astype(BF16), x2, in_place=in_place)


def _nsa_layer(x2, norm, w_in, b_gate, cmp_pe, cmp_w1, cmp_w2, w_out, rope_tabs, final_g, *, batch, seq, final):
    G, DK = NSA_GROUPS, NSA_DK
    hd = NSA_HEADS * DK
    base = 2 * hd + 6 * G * DK
    D = w_in.shape[0]
    src = jnp.arange(3 * NSA_HEADS)
    dst = (src // (3 * NSA_HPG)) * GATE_PAD + src % (3 * NSA_HPG)
    wg = jnp.zeros((D, LANES), F32).at[:, dst].set(w_in[:, base:])
    bg = jnp.zeros((1, LANES), F32).at[0, dst].set(b_gate)
    w = jnp.concatenate([w_in[:, :base], wg], axis=1).astype(BF16)
    qc, qr, kc, vc, ks, kw, vT, z, gate = _nsa_front(x2, norm[None, :], w, bg, *rope_tabs, batch=batch, seq=seq)
    half = CMP_STRIDE * DK
    w1ab = jnp.concatenate([cmp_w1[:, :half, :], cmp_w1[:, half:, :]], axis=2).astype(BF16)
    pe_flat = cmp_pe.reshape(2, 2, 1, half)
    pe2 = jnp.broadcast_to(pe_flat, (2, 2, 8, half)).reshape(2, 16, half).astype(BF16)
    w2p = jnp.zeros((2, cmp_w2.shape[1], LANES), F32).at[:, :, :DK].set(cmp_w2).astype(BF16)
    kcmp, vcmpT = _nsa_cmp(kc, vc, w1ab, pe2, w2p)
    o = _nsa_attn(qc, qr, kcmp, vcmpT, ks, kw, vT, gate, batch=batch, seq=seq)
    return _nsa_out(o, z, w_out.astype(BF16), x2, final_g[None, :], final=final)


def _rope_tables(seq):
    half = ROPE_DIM // 2
    pos = jnp.arange(seq, dtype=F32)
    inv_freq = ROPE_THETA ** (-jnp.arange(0, ROPE_DIM, 2, dtype=F32) / ROPE_DIM)
    ang = pos[:, None] * inv_freq[None, :]
    cos, sin = jnp.cos(ang), jnp.sin(ang)
    lane = jnp.arange(LANES) % NSA_DK
    idx = lane % half
    ctab = jnp.where(lane[None, :] < ROPE_DIM, cos[:, idx], 1.0)
    s1 = jnp.where((lane[None, :] >= half) & (lane[None, :] < ROPE_DIM), sin[:, idx], 0.0)
    s2 = jnp.where(lane[None, :] < half, -sin[:, idx], 0.0)
    return ctab, s1, s2


def kernel(x, ml_norm, ml_w_in, ml_conv_w, ml_conv_b, ml_w_q, ml_w_k, ml_w_v, ml_w_if, ml_b_if, ml_ln_w, ml_skip,
           ml_w_out, nsa_norm, nsa_w_in, nsa_b_gate, nsa_cmp_pe, nsa_cmp_w1, nsa_cmp_w2, nsa_w_out, final_norm):
    batch, seq, d_model = x.shape
    depth = ml_norm.shape[0] + nsa_norm.shape[0]
    assert seq % ML_CHUNK == 0 and seq % NSA_ROWS == 0 and seq % ML_ROWS == 0 and WINDOW % ATT_TILE == 0
    x2 = x.reshape(batch * seq, d_model)
    rope_tabs = _rope_tables(seq)
    for i in range(depth):
        j = i // 2
        if i % 2 == 0:
            x2 = _mlstm_layer(x2, ml_norm[j], ml_w_in[j], ml_conv_w[j], ml_conv_b[j], ml_w_q[j], ml_w_k[j],
                              ml_w_v[j], ml_w_if[j], ml_b_if[j], ml_ln_w[j], ml_skip[j], ml_w_out[j],
                              batch=batch, seq=seq, in_place=(i > 0))
        else:
            x2 = _nsa_layer(x2, nsa_norm[j], nsa_w_in[j], nsa_b_gate[j], nsa_cmp_pe[j], nsa_cmp_w1[j],
                            nsa_cmp_w2[j], nsa_w_out[j], rope_tabs, final_norm, batch=batch, seq=seq,
                            final=(i == depth - 1))
    if depth % 2 == 1:
        x2 = _final_norm(x2, final_norm[None, :])
    return x2.reshape(batch, seq, d_model)
```

```python
import functools

import jax
import jax.numpy as jnp
from jax import lax
from jax.experimental import pallas as pl
from jax.experimental.pallas import tpu as pltpu

F32 = jnp.float32
BF16 = jnp.bfloat16

RMS_EPS = 1e-6
LN_EPS = 1e-6
NEG = -1e30

ML_HEADS = 4
ML_CONV = 4
ML_QKV_BLK = 4
ML_CHUNK = 256
ML_ROWS = 512

NSA_HEADS = 16
NSA_GROUPS = 4
NSA_HPG = NSA_HEADS // NSA_GROUPS
NSA_DK = 64
ROPE_DIM = NSA_DK // 4
ROPE_THETA = 500000.0
CMP_BLOCK = 32
CMP_STRIDE = 16
SEL_BLOCK = 64
SEL_TOPK = 16
SEL_LOCAL = 2
SEL_FORCE = 1e9
WINDOW = 512
NSA_ROWS = 512
ATT_TILE = 256
GATE_PAD = 16
V_ROWS = NSA_DK + 16
LOG2E = 1.4426950408889634

LANES = 128
MXU_TILE = 256
VMEM_LIMIT = 56 * 1024 * 1024


def _dot(a, b):
    return jnp.dot(a, b, preferred_element_type=F32)


def _sigmoid(v):
    return 1.0 / (1.0 + jnp.exp(-v))


def _const_spec(shape):
    zeros = (0,) * len(shape)
    return pl.BlockSpec(shape, lambda *_: zeros, pipeline_mode=pl.Buffered(1))


def _params(sem, flags=None):
    return pltpu.CompilerParams(dimension_semantics=sem, vmem_limit_bytes=VMEM_LIMIT, flags=flags)


def _rmsnorm(x, g):
    ms = jnp.mean(x * x, axis=-1, keepdims=True)
    return (x * lax.rsqrt(ms + RMS_EPS)) * g


def _ml_front_kernel(x_ref, g_ref, win_ref, convw_ref, convb_ref, wq_ref, wk_ref, wv_ref, gx_ref, gi_ref, bif_ref,
                     q_ref, kT_ref, v_ref, xc_ref, z_ref, gcol_ref, grow_ref, xbuf, h_sc,
                     *, tm, tiles_per_seq, inner, q_scale):
    t = pl.program_id(0)
    h_sc[...] = _rmsnorm(x_ref[...], g_ref[...]).astype(BF16)

    @pl.when(t % tiles_per_seq == 0)
    def _():
        xbuf[:, 0:8, :] = jnp.zeros((inner // MXU_TILE, 8, MXU_TILE), F32)

    nchunk = inner // MXU_TILE

    def in_proj(c):
        sl = slice(c * MXU_TILE, (c + 1) * MXU_TILE)
        xbuf[c, 8:8 + tm, :] = _dot(h_sc[...], win_ref[:, sl])
        z_ref[:, sl] = _dot(h_sc[...], win_ref[:, inner + c * MXU_TILE:inner + (c + 1) * MXU_TILE]).astype(BF16)

    in_proj(0)
    gates = jnp.broadcast_to(bif_ref[...], (tm, LANES))
    for c in range(nchunk):
        sl = slice(c * MXU_TILE, (c + 1) * MXU_TILE)
        if c + 1 < nchunk:
            in_proj(c + 1)
        conv = convb_ref[:, sl]
        for tap in range(ML_CONV):
            lo = 8 - (ML_CONV - 1) + tap
            conv = conv + xbuf[c, lo:lo + tm, :] * convw_ref[tap:tap + 1, sl]
        xin_b = xbuf[c, 8:8 + tm, :].astype(BF16)
        xbuf[c, 0:8, :] = xbuf[c, tm:tm + 8, :]
        xc_b = (conv * _sigmoid(conv)).astype(BF16)
        xc_ref[:, sl] = xc_b
        q_ref[:, sl] = (_dot(xc_b, wq_ref[c]) * q_scale).astype(BF16)
        kT_ref[sl, :] = _dot(xc_b, wk_ref[c]).T.astype(BF16)
        v_ref[:, sl] = _dot(xin_b, wv_ref[c]).astype(BF16)
        gates = gates + _dot(xc_b, gx_ref[sl, :]) + _dot(xin_b, gi_ref[sl, :])

    logsig = -(jnp.maximum(-gates, 0.0) + jnp.log1p(jnp.exp(-jnp.abs(gates))))
    lane = lax.broadcasted_iota(jnp.int32, (tm, LANES), 1)
    gsel = jnp.where(lane < ML_HEADS, gates, logsig)
    gcol_ref[...] = gsel[:, 0:2 * ML_HEADS]
    grow_ref[...] = gsel.T[0:2 * ML_HEADS, :]


def _ml_front(x2, g, win, convw, convb, wq, wk, wv, gx, gi, bif, *, seq):
    T, D = x2.shape
    inner = win.shape[1] // 2
    tm = ML_ROWS
    nq = inner // MXU_TILE
    kern = functools.partial(_ml_front_kernel, tm=tm, tiles_per_seq=seq // tm, inner=inner,
                             q_scale=float((inner // ML_HEADS) ** -0.5))
    row = lambda w: pl.BlockSpec((tm, w), lambda t: (t, 0))
    bd = _const_spec((nq, MXU_TILE, MXU_TILE))
    return pl.pallas_call(
        kern,
        grid=(T // tm,),
        in_specs=[row(D), _const_spec((1, D)), _const_spec((D, 2 * inner)),
                  _const_spec((ML_CONV, inner)), _const_spec((1, inner)), bd, bd, bd,
                  _const_spec((inner, LANES)), _const_spec((inner, LANES)), _const_spec((1, LANES))],
        out_specs=[row(inner), pl.BlockSpec((inner, tm), lambda t: (0, t)), row(inner),
                   row(inner), row(inner), row(2 * ML_HEADS),
                   pl.BlockSpec((2 * ML_HEADS, tm), lambda t: (0, t))],
        out_shape=[jax.ShapeDtypeStruct((T, inner), BF16),
                   jax.ShapeDtypeStruct((inner, T), BF16), jax.ShapeDtypeStruct((T, inner), BF16),
                   jax.ShapeDtypeStruct((T, inner), BF16), jax.ShapeDtypeStruct((T, inner), BF16),
                   jax.ShapeDtypeStruct((T, 2 * ML_HEADS), F32), jax.ShapeDtypeStruct((2 * ML_HEADS, T), F32)],
        scratch_shapes=[pltpu.VMEM((nq, tm + 8, MXU_TILE), F32), pltpu.VMEM((tm, D), BF16)],
        compiler_params=_params(("arbitrary",)),
        name="ml_front",
    )(x2, g, win, convw, convb, wq, wk, wv, gx, gi, bif)


def _ml_core_kernel(q_ref, kT_ref, v_ref, gcol_ref, grow_ref, lnw_ref, o_ref, s_sc, n_sc, m_sc,
                    qk_sc, qs_sc, qn_sc, *, L, dh):
    c = pl.program_id(1)
    ones = jnp.ones((L, LANES), BF16)

    @pl.when(c == 0)
    def _():
        s_sc[...] = jnp.zeros_like(s_sc)
        n_sc[...] = jnp.zeros_like(n_sc)
        m_sc[...] = jnp.zeros_like(m_sc)

    gcol = gcol_ref[...]
    grow = grow_ref[...]
    row = lax.broadcasted_iota(jnp.int32, (L, L), 0)
    col = lax.broadcasted_iota(jnp.int32, (L, L), 1)
    causal = col <= row
    def query_matmuls(h):
        sl = slice(h * dh, (h + 1) * dh)
        q = q_ref[:, sl]
        qk_sc[h % 2] = _dot(q, kT_ref[sl, :])
        qs_sc[h % 2] = _dot(q, s_sc[h].astype(BF16))
        qn_sc[h % 2] = _dot(q, n_sc[h].astype(BF16))

    query_matmuls(0)
    for h in range(ML_HEADS):
        if h + 1 < ML_HEADS:
            query_matmuls(h + 1)
        sl = slice(h * dh, (h + 1) * dh)
        v = v_ref[:, sl]
        kT = kT_ref[sl, :]
        fc = gcol[:, ML_HEADS + h:ML_HEADS + h + 1]
        ir = grow[h:h + 1, :]
        fr = grow[ML_HEADS + h:ML_HEADS + h + 1, :]
        m_prev = m_sc[h][:, 0:1]
        b_col = jnp.sum(jnp.where(causal, fr, 0.0), axis=1, keepdims=True)
        b_row = jnp.sum(jnp.where(row <= col, fc, 0.0), axis=0, keepdims=True)
        dmat = jnp.where(causal, b_col - b_row + ir, -jnp.inf)
        inter = b_col + m_prev
        m_row = jnp.maximum(inter, jnp.max(dmat, axis=1, keepdims=True))
        w_intra = jnp.exp(dmat - m_row)
        w_inter = jnp.exp(inter - m_row)
        s = qk_sc[h % 2] * w_intra
        num = _dot(s.astype(BF16), v) + w_inter * qs_sc[h % 2]
        den = jnp.sum(s, axis=1, keepdims=True) + w_inter * qn_sc[h % 2][:, 0:1]
        hh = num * (1.0 / jnp.maximum(jnp.abs(den), jnp.exp(-m_row)))
        mu = jnp.mean(hh, axis=1, keepdims=True)
        cen = hh - mu
        var = jnp.mean(cen * cen, axis=1, keepdims=True)
        o_ref[:, sl] = ((cen * lax.rsqrt(var + LN_EPS)) * lnw_ref[:, sl]).astype(BF16)

        b_last = b_col[L - 1:L, :]
        a_row = b_last - b_row + ir
        m_new = jnp.maximum(b_last + m_prev, jnp.max(a_row, axis=1, keepdims=True))
        decay = jnp.exp(b_last + m_prev - m_new)
        kw = kT * jnp.exp(a_row - m_new).astype(BF16)
        s_sc[h] = decay * s_sc[h] + _dot(kw, v)
        n_sc[h] = decay * n_sc[h] + _dot(kw, ones)
        m_sc[h] = jnp.broadcast_to(m_new, (1, LANES))


def _ml_core(q, kT, v, gcol, grow, lnw, *, batch, seq):
    T, inner = q.shape
    L = ML_CHUNK
    nc = seq // L
    dh = inner // ML_HEADS
    kern = functools.partial(_ml_core_kernel, L=L, dh=dh)
    row = lambda w: pl.BlockSpec((L, w), lambda b, c: (b * nc + c, 0))
    colb = lambda r: pl.BlockSpec((r, L), lambda b, c: (0, b * nc + c))
    return pl.pallas_call(
        kern,
        grid=(batch, nc),
        in_specs=[row(inner), colb(inner), row(inner), row(2 * ML_HEADS), colb(2 * ML_HEADS),
                  _const_spec((1, inner))],
        out_specs=row(inner),
        out_shape=jax.ShapeDtypeStruct((T, inner), BF16),
        scratch_shapes=[pltpu.VMEM((ML_HEADS, dh, dh), F32), pltpu.VMEM((ML_HEADS, dh, LANES), F32),
                        pltpu.VMEM((ML_HEADS, 1, LANES), F32), pltpu.VMEM((2, L, L), F32),
                        pltpu.VMEM((2, L, dh), F32), pltpu.VMEM((2, L, LANES), F32)],
        compiler_params=_params(("arbitrary", "arbitrary")),
        name="ml_core",
    )(q, kT, v, gcol, grow, lnw)


def _ml_out_kernel(hn_ref, xc_ref, z_ref, skip_ref, wout_ref, x_ref, o_ref, u_sc):
    nk = hn_ref.shape[1] // MXU_TILE

    def gated(k):
        sl = slice(k * MXU_TILE, (k + 1) * MXU_TILE)
        z = z_ref[:, sl].astype(F32)
        u = (hn_ref[:, sl].astype(F32) + skip_ref[:, sl] * xc_ref[:, sl].astype(F32)) * (z * _sigmoid(z))
        u_sc[k % 2] = u.astype(BF16)

    gated(0)
    for k in range(nk):
        if k + 1 < nk:
            gated(k + 1)
        part = _dot(u_sc[k % 2], wout_ref[k * MXU_TILE:(k + 1) * MXU_TILE, :])
        o_ref[...] = (x_ref[...] if k == 0 else o_ref[...]) + part


def _ml_out(hn, xc, z, skip, wout, x2, *, in_place):
    T, inner = hn.shape
    D = x2.shape[1]
    tm = ML_ROWS
    row = lambda w: pl.BlockSpec((tm, w), lambda t: (t, 0))
    return pl.pallas_call(
        _ml_out_kernel,
        grid=(T // tm,),
        in_specs=[row(inner), row(inner), row(inner), _const_spec((1, inner)), _const_spec((inner, D)), row(D)],
        out_specs=row(D),
        out_shape=jax.ShapeDtypeStruct((T, D), F32),
        input_output_aliases={5: 0} if in_place else {},
        scratch_shapes=[pltpu.VMEM((2, tm, MXU_TILE), BF16)],
        compiler_params=_params(("arbitrary",)),
        name="ml_out",
    )(hn, xc, z, skip, wout, x2)


def _rope(a, cos, s1, s2):
    return a * cos + pltpu.roll(a, ROPE_DIM // 2, 1) * s1 + pltpu.roll(a, LANES - ROPE_DIM // 2, 1) * s2


def _nsa_front_kernel(x_ref, g_ref, w_ref, bg_ref, cos_ref, s1_ref, s2_ref,
                      qc_ref, qr_ref, kc_ref, vc_ref, ks_ref, kw_ref, vT_ref, z_ref, gate_ref,
                      h_sc, sec_sc, *, tm, hd, gd):
    h_sc[...] = _rmsnorm(x_ref[...], g_ref[...]).astype(BF16)
    cos, s1, s2 = cos_ref[...], s1_ref[...], s2_ref[...]
    scale = NSA_DK ** -0.5 * LOG2E
    G = NSA_GROUPS

    def slabs_of(sec, rope):
        slabs = [sec[c] for c in range(MXU_TILE // LANES)]
        return [_rope(a, cos, s1, s2) for a in slabs] if rope else slabs

    def store_q(c):
        def run(sec):
            for half, a in enumerate(slabs_of(sec, False)):
                sl = slice(c * MXU_TILE + half * LANES, c * MXU_TILE + (half + 1) * LANES)
                qc_ref[:, sl] = (a * scale).astype(BF16)
                qr_ref[:, sl] = (_rope(a, cos, s1, s2) * scale).astype(BF16)
        return run

    def store_rows(ref, rope):
        def run(sec):
            slabs = slabs_of(sec, rope)
            for g in range(G):
                lo = (g * NSA_DK) % LANES
                ref[0, g] = slabs[(g * NSA_DK) // LANES][:, lo:lo + NSA_DK].astype(ref.dtype)
        return run

    def store_blocks(ref):
        def run(sec):
            for r in range(CMP_STRIDE):
                for c in range(MXU_TILE // LANES):
                    rows = sec[c, pl.ds(r, tm // CMP_STRIDE, stride=CMP_STRIDE), :]
                    for half in range(LANES // NSA_DK):
                        g = c * (LANES // NSA_DK) + half
                        ref[0, g, :, r * NSA_DK:(r + 1) * NSA_DK] = (
                            rows[:, half * NSA_DK:(half + 1) * NSA_DK].astype(BF16))
        return run

    def store_cols(br):
        def run(sec):
            slabs = slabs_of(sec, False)
            ones = jnp.ones((V_ROWS - NSA_DK, ATT_TILE), BF16)
            for g in range(G):
                aT = slabs[(g * NSA_DK) // LANES].T
                lo = (g * NSA_DK) % LANES
                for j in range(tm // ATT_TILE):
                    vT_ref[0, g, br, j, 0:NSA_DK, :] = (
                        aT[lo:lo + NSA_DK, j * ATT_TILE:(j + 1) * ATT_TILE].astype(BF16))
                    vT_ref[0, g, br, j, NSA_DK:V_ROWS, :] = ones
        return run

    def store_z(c):
        def run(sec):
            for half in range(MXU_TILE // LANES):
                z_ref[:, c * MXU_TILE + half * LANES:c * MXU_TILE + (half + 1) * LANES] = sec[half].astype(BF16)
        return run

    def store_gate(sec):
        gT = _sigmoid(sec[0] + bg_ref[...]).T
        for g in range(G):
            for j in range(tm // ATT_TILE):
                gate_ref[g, j] = gT[g * GATE_PAD:(g + 1) * GATE_PAD, j * ATT_TILE:(j + 1) * ATT_TILE]

    zoff = hd + 6 * gd
    sections = ([(c * MXU_TILE, MXU_TILE, store_q(c)) for c in range(hd // MXU_TILE)]
                + [(hd, gd, store_blocks(kc_ref)), (hd + gd, gd, store_blocks(vc_ref)),
                   (hd + 2 * gd, gd, store_rows(ks_ref, True)), (hd + 3 * gd, gd, store_cols(0)),
                   (hd + 4 * gd, gd, store_rows(kw_ref, True)), (hd + 5 * gd, gd, store_cols(1))]
                + [(zoff + c * MXU_TILE, MXU_TILE, store_z(c)) for c in range(hd // MXU_TILE)]
                + [(zoff + hd, LANES, store_gate)])

    def project(k):
        off, width, _ = sections[k]
        res = _dot(h_sc[...], w_ref[:, off:off + width])
        for c in range(width // LANES):
            sec_sc[k % 2, c] = res[:, c * LANES:(c + 1) * LANES]

    project(0)
    for k, (_, _, consume) in enumerate(sections):
        if k + 1 < len(sections):
            project(k + 1)
        consume(sec_sc.at[k % 2])


def _nsa_front(x2, g, w, bg, cos, s1, s2, *, batch, seq):
    T, D = x2.shape
    tm = NSA_ROWS
    tps = seq // tm
    G = NSA_GROUPS
    hd = NSA_HEADS * NSA_DK
    gd = G * NSA_DK
    ntk = seq // ATT_TILE
    kern = functools.partial(_nsa_front_kernel, tm=tm, hd=hd, gd=gd)
    row = lambda w_: pl.BlockSpec((tm, w_), lambda t: (t, 0))
    tab = pl.BlockSpec((tm, LANES), lambda t: (t % tps, 0))
    rows4 = pl.BlockSpec((1, G, tm, NSA_DK), lambda t: (t // tps, 0, t % tps, 0))
    cols6 = pl.BlockSpec((1, G, 2, tm // ATT_TILE, V_ROWS, ATT_TILE), lambda t: (t // tps, 0, 0, t % tps, 0, 0))
    rows_shape = jax.ShapeDtypeStruct((batch, G, seq, NSA_DK), BF16)
    blk_w = CMP_STRIDE * NSA_DK
    blocks4 = pl.BlockSpec((1, G, tm // CMP_STRIDE, blk_w), lambda t: (t // tps, 0, t % tps, 0))
    blocks_shape = jax.ShapeDtypeStruct((batch, G, seq // CMP_STRIDE, blk_w), BF16)
    cols_shape = jax.ShapeDtypeStruct((batch, G, 2, ntk, V_ROWS, ATT_TILE), BF16)
    return pl.pallas_call(
        kern,
        grid=(T // tm,),
        in_specs=[row(D), _const_spec((1, D)), _const_spec(w.shape), _const_spec((1, LANES)), tab, tab, tab],
        out_specs=[row(hd), row(hd), blocks4, blocks4, rows4, rows4, cols6, row(hd),
                   pl.BlockSpec((G, tm // ATT_TILE, GATE_PAD, ATT_TILE), lambda t: (0, t, 0, 0))],
        out_shape=[jax.ShapeDtypeStruct((T, hd), BF16), jax.ShapeDtypeStruct((T, hd), BF16),
                   blocks_shape, blocks_shape, rows_shape, rows_shape, cols_shape,
                   jax.ShapeDtypeStruct((T, hd), BF16),
                   jax.ShapeDtypeStruct((G, T // ATT_TILE, GATE_PAD, ATT_TILE), F32)],
        scratch_shapes=[pltpu.VMEM((tm, D), BF16), pltpu.VMEM((2, MXU_TILE // LANES, tm, LANES), F32)],
        compiler_params=_params(("arbitrary",)),
        name="nsa_front",
    )(x2, g, w, bg, cos, s1, s2)


def _nsa_cmp_kernel(kc_ref, vc_ref, w1_ref, pe_ref, w2_ref, kcmp_ref, vcmpT_ref, *, hid, n):
    def mlp(x_ref, i):
        ab = _dot(x_ref[0, 0], w1_ref[i])
        pb = _dot(pe_ref[i], w1_ref[i])
        bias = pb[0:1, 0:hid] + pb[8:9, hid:2 * hid]
        h1 = ab[:, 0:hid] + pltpu.roll(ab[:, hid:2 * hid], n - 1, 0) + bias
        return _dot((h1 * _sigmoid(h1)).astype(BF16), w2_ref[i])

    kcmp_ref[0, 0] = mlp(kc_ref, 0)[:, 0:NSA_DK].astype(BF16)
    vcmpT_ref[0, 0] = mlp(vc_ref, 1).T[0:NSA_DK, :].astype(BF16)


def _nsa_cmp(kc, vc, w1ab, pe2, w2p):
    B, G, n, width = kc.shape
    hid = w1ab.shape[2] // 2
    blk = pl.BlockSpec((1, 1, n, width), lambda b, g: (b, g, 0, 0))
    return pl.pallas_call(
        functools.partial(_nsa_cmp_kernel, hid=hid, n=n),
        grid=(B, G),
        in_specs=[blk, blk, _const_spec(w1ab.shape), _const_spec(pe2.shape), _const_spec(w2p.shape)],
        out_specs=[pl.BlockSpec((1, 1, n, NSA_DK), lambda b, g: (b, g, 0, 0)),
                   pl.BlockSpec((1, 1, NSA_DK, n), lambda b, g: (b, g, 0, 0))],
        out_shape=[jax.ShapeDtypeStruct((B, G, n, NSA_DK), BF16), jax.ShapeDtypeStruct((B, G, NSA_DK, n), BF16)],
        compiler_params=_params(("arbitrary", "arbitrary")),
        name="nsa_cmp",
    )(kc, vc, w1ab, pe2, w2p)


def _nsa_attn_kernel(qc_ref, qr_ref, kcmp_ref, vcmpT_ref, ks_ref, kw_ref, vT_ref, gate_ref, o_ref,
                     ksa_sc, qaug_sc, s_a, s_b, p_a, p_b, sw_sc, m_sc, alpha_sc, acc_sc, *, tq, seq, i):
    H = NSA_HPG
    W = H * tq
    tk = tq
    nsel = seq // SEL_BLOCK
    ncmp = kcmp_ref.shape[2]
    nwin = WINDOW // tk + 1

    @pl.when(i == 0)
    def _():
        ksa_sc[:, 0:NSA_DK] = ks_ref[0, 0]
        kb = lax.broadcasted_iota(jnp.int32, (seq, LANES - NSA_DK), 0) // SEL_BLOCK
        nn = lax.broadcasted_iota(jnp.int32, (seq, LANES - NSA_DK), 1)
        ksa_sc[:, NSA_DK:LANES] = jnp.where(kb == nn, 1.0, 0.0).astype(BF16)

    def heads_on_lanes(q_tile):
        qT = q_tile.astype(F32).T
        return jnp.concatenate([qT[h * NSA_DK:(h + 1) * NSA_DK, :] for h in range(H)], axis=1)

    def key_rows(ref2d, j):
        return ref2d[pl.ds(pl.multiple_of(j * tk, tk), tk), :]

    qcT = heads_on_lanes(qc_ref[...]).astype(BF16)
    qrT = heads_on_lanes(qr_ref[...]).astype(BF16)
    delta = (lax.broadcasted_iota(jnp.int32, (tk, W), 0)
             - (lax.broadcasted_iota(jnp.int32, (tk, W), 1) & (tq - 1)))

    sc = _dot(kcmp_ref[0, 0], qcT)
    kw2 = kw_ref.at[0, 0]
    for d in range(nwin):
        sw_sc[d] = _dot(key_rows(kw2, jnp.maximum(i - d, 0)), qrT)

    cidx = lax.broadcasted_iota(jnp.int32, (ncmp, W), 0)
    qpos = i * tq + (lax.broadcasted_iota(jnp.int32, (ncmp, W), 1) & (tq - 1))
    sc = jnp.where(cidx * CMP_STRIDE + (CMP_BLOCK - 1) <= qpos, sc, NEG)
    p = jnp.exp2(sc - jnp.max(sc, axis=0, keepdims=True))
    p = p * (1.0 / jnp.sum(p, axis=0, keepdims=True))
    p = p * jnp.where(qpos[0:1, :] >= CMP_BLOCK - 1, 1.0, 0.0)
    ocmp = _dot(vcmpT_ref[0, 0], p.astype(BF16))

    psum = p[:, 0:tq]
    for h in range(1, H):
        psum = psum + p[:, h * tq:(h + 1) * tq]
    nn = lax.broadcasted_iota(jnp.int32, (nsel, ncmp), 0) * SEL_BLOCK
    cc = lax.broadcasted_iota(jnp.int32, (nsel, ncmp), 1) * CMP_STRIDE
    ov = jnp.maximum(jnp.minimum(cc + CMP_BLOCK, nn + SEL_BLOCK) - jnp.maximum(cc, nn), 0)
    ov = (ov.astype(F32) * (1.0 / CMP_STRIDE)).astype(BF16)
    p_hi = psum.astype(BF16)
    r1 = psum - p_hi.astype(F32)
    p_mid = r1.astype(BF16)
    p_lo = (r1 - p_mid.astype(F32)).astype(BF16)
    imp = _dot(ov, p_hi) + _dot(ov, p_mid) + _dot(ov, p_lo)

    nidx = lax.broadcasted_iota(jnp.int32, (nsel, tq), 0)
    qblk = (i * tq + lax.broadcasted_iota(jnp.int32, (nsel, tq), 1)) // SEL_BLOCK
    dist = qblk - nidx
    forced = (nidx == 0) | ((dist >= 0) & (dist < SEL_LOCAL))
    imp = jnp.where(forced, SEL_FORCE, jnp.where(dist >= 0, imp, -1.0))
    rank = jnp.zeros((nsel, tq), jnp.int32)
    for m in range(nsel):
        rm = imp[m:m + 1, :]
        before = (rm > imp) | ((rm == imp) & (nidx > m))
        rank = rank + before.astype(jnp.int32)
    sel = (rank < SEL_TOPK) & (dist >= 0)
    bias = jnp.where(sel, 0.0, NEG).astype(BF16)
    bias = jnp.concatenate([bias] * H, axis=1)
    qaug = jnp.concatenate([qrT, bias, jnp.zeros((LANES - NSA_DK - nsel, W), BF16)], axis=0)
    qaug_sc[...] = qaug

    s_b[...] = _dot(key_rows(ksa_sc, i), qaug)
    s_a[...] = _dot(key_rows(ksa_sc, 0), qaug)

    far_thr = jnp.where(i >= nwin - 1, 0, 1 << 20)
    alpha_sc[...] = jnp.ones((1, W), F32)
    acc_sc[...] = jnp.zeros((V_ROWS, W), F32)
    half_w = W // 2
    owin_halves = []
    for hb in range(2):
        ls = slice(hb * half_w, (hb + 1) * half_w)
        dl = delta[:, ls]
        s_edge = jnp.where(dl <= 0, sw_sc[0, :, ls], sw_sc[nwin - 1, :, ls])
        m_w = jnp.max(s_edge, axis=0, keepdims=True)
        for d in range(1, nwin - 1):
            m_w = jnp.maximum(m_w, jnp.where(i >= d, jnp.max(sw_sc[d, :, ls], axis=0, keepdims=True), NEG))
        pt = jnp.exp2(s_edge - m_w)
        acc_w = _dot(vT_ref[0, 0, 1, i], jnp.where(dl <= 0, pt, 0.0).astype(BF16))
        acc_w = acc_w + _dot(vT_ref[0, 0, 1, jnp.maximum(i - (nwin - 1), 0)],
                             jnp.where(dl > far_thr, pt, 0.0).astype(BF16))
        for d in range(1, nwin - 1):
            pt = jnp.exp2(sw_sc[d, :, ls] - (m_w + jnp.where(i >= d, 0.0, -NEG)))
            acc_w = acc_w + _dot(vT_ref[0, 0, 1, jnp.maximum(i - d, 0)], pt.astype(BF16))
        owin_halves.append(acc_w[0:NSA_DK, :] * (1.0 / acc_w[V_ROWS - 1:V_ROWS, :]))

        s = jnp.where(dl <= 0, s_b[:, ls], NEG)
        m0 = jnp.max(s, axis=0, keepdims=True)
        m_sc[:, ls] = m0
        p_b[:, ls] = jnp.exp2(s - m0).astype(BF16)
    owin = jnp.concatenate(owin_halves, axis=1)

    def value_stage(jm, p_ref):
        vT = vT_ref[0, 0, 0, jm]
        for h in range(H):
            ls = slice(h * tq, (h + 1) * tq)
            acc_sc[:, ls] = alpha_sc[:, ls] * acc_sc[:, ls] + _dot(vT, p_ref[:, ls])

    def stages(j, s_cur, p_cur, s_nxt, p_prv):
        vT = vT_ref[0, 0, 0, jnp.where(j == 0, i, j - 1)]
        k_nxt = key_rows(ksa_sc, jnp.minimum(j + 1, i - 1))
        for h in range(H):
            ls = slice(h * tq, (h + 1) * tq)
            acc_sc[:, ls] = alpha_sc[:, ls] * acc_sc[:, ls] + _dot(vT, p_prv[:, ls])
            m_old = m_sc[:, ls]
            m_new = jnp.maximum(m_old, jnp.max(s_cur[:, ls], axis=0, keepdims=True))
            m_sc[:, ls] = m_new
            alpha_sc[:, ls] = jnp.exp2(m_old - m_new)
            p_cur[:, ls] = jnp.exp2(s_cur[:, ls] - m_new).astype(BF16)
            s_nxt[:, ls] = _dot(k_nxt, qaug_sc[:, ls])

    def sel_body(j, carry):
        @pl.when((j & 1) == 0)
        def _():
            stages(j, s_a, p_a, s_b, p_b)

        @pl.when((j & 1) == 1)
        def _():
            stages(j, s_b, p_b, s_a, p_a)

        return carry

    lax.fori_loop(0, i, sel_body, 0)

    @pl.when((i & 1) == 0)
    def _():
        value_stage(jnp.where(i == 0, i, i - 1), p_b)

    @pl.when((i & 1) == 1)
    def _():
        value_stage(i - 1, p_a)

    acc = acc_sc[...]
    osel = acc[0:NSA_DK, :] * (1.0 / acc[V_ROWS - 1:V_ROWS, :])

    gate = gate_ref[0]
    outs = []
    for h in range(H):
        ls = slice(h * tq, (h + 1) * tq)
        outs.append(gate[3 * h:3 * h + 1, :] * ocmp[:, ls] + gate[3 * h + 1:3 * h + 2, :] * osel[:, ls]
                    + gate[3 * h + 2:3 * h + 3, :] * owin[:, ls])
    o_ref[...] = jnp.concatenate(outs, axis=0).T.astype(BF16)


def _nsa_attn_group_kernel(qc_ref, qr_ref, kcmp_ref, vcmpT_ref, ks_ref, kw_ref, vT_ref, gate_ref, o_ref, *scratch,
                           tq, seq):
    def tile(i, carry):
        rows = pl.ds(pl.multiple_of(i * tq, tq), tq)
        _nsa_attn_kernel(qc_ref.at[rows, :], qr_ref.at[rows, :], kcmp_ref, vcmpT_ref, ks_ref, kw_ref, vT_ref,
                         gate_ref.at[:, i], o_ref.at[rows, :], *scratch, tq=tq, seq=seq, i=i)
        return carry

    lax.fori_loop(0, seq // tq, tile, 0)


def _nsa_attn(qc, qr, kcmp, vcmpT, ks, kw, vT, gate, *, batch, seq):
    T, hd = qc.shape
    G = NSA_GROUPS
    tq = ATT_TILE
    nq = seq // tq
    gw = NSA_HPG * NSA_DK
    W = NSA_HPG * tq
    ncmp = kcmp.shape[2]
    nwin = WINDOW // tq + 1
    qspec = pl.BlockSpec((seq, gw), lambda b, g: (b, g))
    rows4 = pl.BlockSpec((1, 1, seq, NSA_DK), lambda b, g: (b, g, 0, 0))
    return pl.pallas_call(
        functools.partial(_nsa_attn_group_kernel, tq=tq, seq=seq),
        grid=(batch, G),
        in_specs=[qspec, qspec,
                  pl.BlockSpec((1, 1, ncmp, NSA_DK), lambda b, g: (b, g, 0, 0)),
                  pl.BlockSpec((1, 1, NSA_DK, ncmp), lambda b, g: (b, g, 0, 0)),
                  rows4, rows4,
                  pl.BlockSpec((1, 1, 2, nq, V_ROWS, tq), lambda b, g: (b, g, 0, 0, 0, 0)),
                  pl.BlockSpec((1, nq, GATE_PAD, tq), lambda b, g: (g, b, 0, 0))],
        out_specs=qspec,
        out_shape=jax.ShapeDtypeStruct((T, hd), BF16),
        scratch_shapes=[pltpu.VMEM((seq, LANES), BF16), pltpu.VMEM((LANES, W), BF16),
                        pltpu.VMEM((tq, W), F32), pltpu.VMEM((tq, W), F32),
                        pltpu.VMEM((tq, W), BF16), pltpu.VMEM((tq, W), BF16), pltpu.VMEM((nwin, tq, W), F32),
                        pltpu.VMEM((1, W), F32), pltpu.VMEM((1, W), F32), pltpu.VMEM((V_ROWS, W), F32)],
        compiler_params=_params(("arbitrary", "arbitrary")),
        name="nsa_attn",
    )(qc, qr, kcmp, vcmpT, ks, kw, vT, gate)


def _nsa_out_kernel(o_ref, z_ref, wout_ref, x_ref, gf_ref, y_ref, u_sc, *, final):
    nk = o_ref.shape[1] // MXU_TILE

    def gated(k):
        sl = slice(k * MXU_TILE, (k + 1) * MXU_TILE)
        z = z_ref[:, sl].astype(F32)
        u_sc[k % 2] = (o_ref[:, sl].astype(F32) * (z * _sigmoid(z))).astype(BF16)

    gated(0)
    for k in range(nk):
        if k + 1 < nk:
            gated(k + 1)
        part = _dot(u_sc[k % 2], wout_ref[k * MXU_TILE:(k + 1) * MXU_TILE, :])
        y = (x_ref[...] if k == 0 else y_ref[...]) + part
        y_ref[...] = _rmsnorm(y, gf_ref[...]) if (final and k == nk - 1) else y


def _nsa_out(o, z, wout, x2, gf, *, final):
    T, hd = o.shape
    D = x2.shape[1]
    tm = NSA_ROWS
    row = lambda w: pl.BlockSpec((tm, w), lambda t: (t, 0))
    return pl.pallas_call(
        functools.partial(_nsa_out_kernel, final=final),
        grid=(T // tm,),
        in_specs=[row(hd), row(hd), _const_spec((hd, D)), row(D), _const_spec((1, D))],
        out_specs=row(D),
        out_shape=jax.ShapeDtypeStruct((T, D), F32),
        input_output_aliases={3: 0},
        scratch_shapes=[pltpu.VMEM((2, tm, MXU_TILE), BF16)],
        compiler_params=_params(("arbitrary",)),
        name="nsa_out",
    )(o, z, wout, x2, gf)


def _final_norm_kernel(x_ref, g_ref, y_ref):
    y_ref[...] = _rmsnorm(x_ref[...], g_ref[...])


def _final_norm(x2, gf):
    T, D = x2.shape
    tm = NSA_ROWS
    row = pl.BlockSpec((tm, D), lambda t: (t, 0))
    return pl.pallas_call(
        _final_norm_kernel, grid=(T // tm,), in_specs=[row, _const_spec((1, D))], out_specs=row,
        out_shape=jax.ShapeDtypeStruct((T, D), F32), compiler_params=_params(("arbitrary",)),
        name="final_norm",
    )(x2, gf)


def _block_diag_tiles(w):
    nblk = w.shape[0]
    per = MXU_TILE // ML_QKV_BLK
    rows = jnp.swapaxes(w, 1, 2).reshape(nblk // per, MXU_TILE, ML_QKV_BLK)
    tiled = jnp.tile(rows, (1, 1, per))
    blk = jnp.arange(MXU_TILE) // ML_QKV_BLK
    return jnp.where(blk[:, None] == blk[None, :], tiled, 0.0).astype(BF16)


def _fold_headwise(w, w_if_part):
    nblk = w.shape[0]
    folded = jnp.einsum('nij,nio->njo', w, w_if_part.reshape(nblk, ML_QKV_BLK, -1), precision='highest')
    return folded.reshape(nblk * ML_QKV_BLK, -1)


def _mlstm_layer(x2, norm, w_in, conv_w, conv_b, w_q, w_k, w_v, w_if, b_if, ln_w, skip, w_out,
                 *, batch, seq, in_place):
    inner = w_in.shape[1] // 2
    pad = lambda m: jnp.zeros((m.shape[0], LANES), F32).at[:, :2 * ML_HEADS].set(m)
    gx = pad(_fold_headwise(w_q, w_if[:inner]) + _fold_headwise(w_k, w_if[inner:2 * inner])).astype(BF16)
    gi = pad(_fold_headwise(w_v, w_if[2 * inner:])).astype(BF16)
    q, kT, v, xc, z, gcol, grow = _ml_front(
        x2, norm[None, :], w_in.astype(BF16), conv_w, conv_b[None, :],
        _block_diag_tiles(w_q), _block_diag_tiles(w_k), _block_diag_tiles(w_v), gx, gi, pad(b_if[None, :]), seq=seq)
    hn = _ml_core(q, kT, v, gcol, grow, ln_w[None, :], batch=batch, seq=seq)
    return _ml_out(hn, xc, z, skip[None, :], w_out.astype(BF16), x2, in_place=in_place)


def _nsa_layer(x2, norm, w_in, b_gate, cmp_pe, cmp_w1, cmp_w2, w_out, rope_tabs, final_g, *, batch, seq, final):
    G, DK = NSA_GROUPS, NSA_DK
    hd = NSA_HEADS * DK
    base = 2 * hd + 6 * G * DK
    D = w_in.shape[0]
    src = jnp.arange(3 * NSA_HEADS)
    dst = (src // (3 * NSA_HPG)) * GATE_PAD + src % (3 * NSA_HPG)
    wg = jnp.zeros((D, LANES), F32).at[:, dst].set(w_in[:, base:])
    bg = jnp.zeros((1, LANES), F32).at[0, dst].set(b_gate)
    w = jnp.concatenate([w_in[:, :base], wg], axis=1).astype(BF16)
    qc, qr, kc, vc, ks, kw, vT, z, gate = _nsa_front(x2, norm[None, :], w, bg, *rope_tabs, batch=batch, seq=seq)
    half = CMP_STRIDE * DK
    w1ab = jnp.concatenate([cmp_w1[:, :half, :], cmp_w1[:, half:, :]], axis=2).astype(BF16)
    pe_flat = cmp_pe.reshape(2, 2, 1, half)
    pe2 = jnp.broadcast_to(pe_flat, (2, 2, 8, half)).reshape(2, 16, half).astype(BF16)
    w2p = jnp.zeros((2, cmp_w2.shape[1], LANES), F32).at[:, :, :DK].set(cmp_w2).astype(BF16)
    kcmp, vcmpT = _nsa_cmp(kc, vc, w1ab, pe2, w2p)
    o = _nsa_attn(qc, qr, kcmp, vcmpT, ks, kw, vT, gate, batch=batch, seq=seq)
    return _nsa_out(o, z, w_out.astype(BF16), x2, final_g[None, :], final=final)


def _rope_tables(seq):
    half = ROPE_DIM // 2
    pos = jnp.arange(seq, dtype=F32)
    inv_freq = ROPE_THETA ** (-jnp.arange(0, ROPE_DIM, 2, dtype=F32) / ROPE_DIM)
    ang = pos[:, None] * inv_freq[None, :]
    cos, sin = jnp.cos(ang), jnp.sin(ang)
    lane = jnp.arange(LANES) % NSA_DK
    idx = lane % half
    ctab = jnp.where(lane[None, :] < ROPE_DIM, cos[:, idx], 1.0)
    s1 = jnp.where((lane[None, :] >= half) & (lane[None, :] < ROPE_DIM), sin[:, idx], 0.0)
    s2 = jnp.where(lane[None, :] < half, -sin[:, idx], 0.0)
    return ctab, s1, s2


def kernel(x, ml_norm, ml_w_in, ml_conv_w, ml_conv_b, ml_w_q, ml_w_k, ml_w_v, ml_w_if, ml_b_if, ml_ln_w, ml_skip,
           ml_w_out, nsa_norm, nsa_w_in, nsa_b_gate, nsa_cmp_pe, nsa_cmp_w1, nsa_cmp_w2, nsa_w_out, final_norm):
    batch, seq, d_model = x.shape
    depth = ml_norm.shape[0] + nsa_norm.shape[0]
    assert seq % ML_CHUNK == 0 and seq % NSA_ROWS == 0 and seq % ML_ROWS == 0 and WINDOW % ATT_TILE == 0
    x2 = x.reshape(batch * seq, d_model)
    rope_tabs = _rope_tables(seq)
    for i in range(depth):
        j = i // 2
        if i % 2 == 0:
            x2 = _mlstm_layer(x2, ml_norm[j], ml_w_in[j], ml_conv_w[j], ml_conv_b[j], ml_w_q[j], ml_w_k[j],
                              ml_w_v[j], ml_w_if[j], ml_b_if[j], ml_ln_w[j], ml_skip[j], ml_w_out[j],
                              batch=batch, seq=seq, in_place=(i > 0))
        else:
            x2 = _nsa_layer(x2, nsa_norm[j], nsa_w_in[j], nsa_b_gate[j], nsa_cmp_pe[j], nsa_cmp_w1[j],
                            nsa_cmp_w2[j], nsa_w_out[j], rope_tabs, final_norm, batch=batch, seq=seq,
                            final=(i == depth - 1))
    if depth % 2 == 1:
        x2 = _final_norm(x2, final_norm[None, :])
    return x2.reshape(batch, seq, d_model)
```
